```python
import math
import jax, jax.numpy as jnp
from jax import lax
import numpy as np

D_MODEL = 2048
BATCH = 1
SEQ = 16384
DEPTH = 1

GLA_HEADS = 4
GLA_DK = 128
GLA_DV = 256
GLA_RANK = 16
GLA_TAU = 16.0
GLA_CHUNK = 64
DIFF_HEADS = 4
DIFF_DQK = 128
DIFF_DV = 2 * DIFF_DQK
Q_BLOCK = 128
D_FF = 5632
CONV_W = 3
EPS = 1e-6

GLA_QK = GLA_HEADS * GLA_DK
GLA_V = GLA_HEADS * GLA_DV
DIFF_QK = DIFF_HEADS * 2 * DIFF_DQK
DIFF_V = DIFF_HEADS * DIFF_DV
MIX_WIDTH = GLA_V + DIFF_V
IN_SPLITS = (GLA_QK, GLA_QK, GLA_V, GLA_V, GLA_RANK, DIFF_QK, DIFF_QK, DIFF_V)
IN_COLS = sum(IN_SPLITS)

kernel_name = 'hybrid_gla_diffattn_convffn'


def rmsnorm(x, g):
    xf = x.astype(jnp.float32)
    y = xf * lax.rsqrt(jnp.mean(xf * xf, axis=-1, keepdims=True) + EPS)
    return (y * g.astype(jnp.float32)).astype(x.dtype)


def gla_chunked(q, k, v, log_a):
    B, S = q.shape[0], q.shape[1]
    n = S // GLA_CHUNK

    def to_chunks(t):
        t = t.astype(jnp.float32).reshape(B, n, GLA_CHUNK, GLA_HEADS, t.shape[-1])
        return jnp.transpose(t, (1, 0, 3, 2, 4))

    qc = to_chunks(q * (GLA_DK ** -0.5))
    kc, vc, ac = to_chunks(k), to_chunks(v), to_chunks(log_a)
    causal = jnp.tril(jnp.ones((GLA_CHUNK, GLA_CHUNK), dtype=bool))

    def step(state, inp):
        qi, ki, vi, ai = inp
        b = jnp.cumsum(ai, axis=2)
        o_inter = jnp.einsum('bhcd,bhde->bhce', qi * jnp.exp(b), state)
        diff = b[:, :, :, None, :] - b[:, :, None, :, :]
        decay = jnp.exp(jnp.where(causal[:, :, None], diff, -jnp.inf))
        scores = jnp.einsum('bhid,bhjd,bhijd->bhij', qi, ki, decay)
        o_intra = jnp.einsum('bhij,bhje->bhie', scores, vi)
        b_last = b[:, :, -1:, :]
        state = (jnp.exp(b_last[:, :, 0, :])[..., None] * state
                 + jnp.einsum('bhcd,bhce->bhde', ki * jnp.exp(b_last - b), vi))
        return state, o_inter + o_intra

    s0 = jnp.zeros((B, GLA_HEADS, GLA_DK, GLA_DV), jnp.float32)
    _, o = lax.scan(step, s0, (qc, kc, vc, ac))
    return jnp.transpose(o, (1, 0, 3, 2, 4)).reshape(B, S, GLA_HEADS, GLA_DV)


def diff_attention(q, k, v, lam):
    B, S = q.shape[0], q.shape[1]
    nb = S // Q_BLOCK
    slopes = jnp.asarray(2.0 ** (-8.0 * np.arange(1, DIFF_HEADS + 1) / DIFF_HEADS), jnp.float32)
    qb = jnp.transpose(q.reshape(B, nb, Q_BLOCK, DIFF_HEADS, 2, DIFF_DQK), (1, 0, 2, 3, 4, 5))
    k_pos = jnp.arange(S, dtype=jnp.int32)
    scale = DIFF_DQK ** -0.5

    def block(args):
        qi, bi = args
        q_pos = bi * Q_BLOCK + jnp.arange(Q_BLOCK, dtype=jnp.int32)
        dist = q_pos[:, None] - k_pos[None, :]
        bias = -slopes[:, None, None] * dist.astype(jnp.float32)
        s = jnp.einsum('bqhcd,bkhcd->bhcqk', qi, k,
                       preferred_element_type=jnp.float32) * scale + bias[None, :, None]
        s = jnp.where(dist >= 0, s, -jnp.inf)
        p = jax.nn.softmax(s, axis=-1)
        a = p[:, :, 0] - lam * p[:, :, 1]
        return jnp.einsum('bhqk,bkhe->bqhe', a, v.astype(jnp.float32))

    o = lax.map(block, (qb, jnp.arange(nb, dtype=jnp.int32)))
    return jnp.transpose(o, (1, 0, 2, 3, 4)).reshape(B, S, DIFF_HEADS, DIFF_DV)


def token_mixer(h, w_in, w_alpha_up, b_alpha, gla_norm, lambda_q1, lambda_k1,
                lambda_q2, lambda_k2, diff_norm, w_o, lambda_init):
    B, S, _ = h.shape
    proj = h @ w_in
    offs = np.cumsum((0,) + IN_SPLITS)
    gq, gk, gv, gg, ga, dq, dk, dv = [proj[..., int(offs[i]):int(offs[i + 1])]
                                      for i in range(len(IN_SPLITS))]
    log_a = jax.nn.log_sigmoid((ga @ w_alpha_up + b_alpha).astype(jnp.float32)) / GLA_TAU
    o_a = gla_chunked(gq.reshape(B, S, GLA_HEADS, GLA_DK), gk.reshape(B, S, GLA_HEADS, GLA_DK),
                      gv.reshape(B, S, GLA_HEADS, GLA_DV), log_a.reshape(B, S, GLA_HEADS, GLA_DK))
    gate = jax.nn.silu(gg.astype(jnp.float32)).reshape(B, S, GLA_HEADS, GLA_DV)
    o_a = (rmsnorm(o_a, gla_norm) * gate).reshape(B, S, GLA_V)
    f32 = jnp.float32
    lam = (jnp.exp(jnp.sum(lambda_q1.astype(f32) * lambda_k1.astype(f32)))
           - jnp.exp(jnp.sum(lambda_q2.astype(f32) * lambda_k2.astype(f32))) + lambda_init)
    o_b = diff_attention(dq.reshape(B, S, DIFF_HEADS, 2, DIFF_DQK),
                         dk.reshape(B, S, DIFF_HEADS, 2, DIFF_DQK),
                         dv.reshape(B, S, DIFF_HEADS, DIFF_DV), lam)
    o_b = (rmsnorm(o_b, diff_norm) * (1.0 - lambda_init)).reshape(B, S, DIFF_V)
    o = jnp.concatenate([o_a, o_b], axis=-1).astype(h.dtype)
    return o @ w_o


def conv_ffn(h, w_ffn_in, conv_w, conv_b, w_ffn_out):
    S = h.shape[1]
    up = h @ w_ffn_in
    a, b = up[..., :D_FF], up[..., D_FF:]
    a_pad = jnp.pad(a, ((0, 0), (CONV_W - 1, 0), (0, 0)))
    a = (conv_w[0] * a_pad[:, 0:S] + conv_w[1] * a_pad[:, 1:S + 1]
         + conv_w[2] * a_pad[:, 2:S + 2] + conv_b)
    return (jax.nn.gelu(a, approximate=True) * b) @ w_ffn_out


def setup_inputs(seed: int = 0) -> dict:
    key = jax.random.key(seed)
    ks = jax.random.split(key, 20)
    nrm = lambda k, shape: jax.random.normal(k, shape, jnp.float32)
    gain = lambda k, n: 1.0 + 0.1 * nrm(k, (DEPTH, n))
    return {
        'x': nrm(ks[0], (BATCH, SEQ, D_MODEL)),
        'attn_pre_norm': gain(ks[1], D_MODEL),
        'w_in': nrm(ks[2], (DEPTH, D_MODEL, IN_COLS)) * D_MODEL ** -0.5,
        'w_alpha_up': nrm(ks[3], (DEPTH, GLA_RANK, GLA_QK)) * GLA_RANK ** -0.5,
        'b_alpha': 0.1 * nrm(ks[4], (DEPTH, GLA_QK)),
        'gla_norm': gain(ks[5], GLA_DV),
        'lambda_q1': 0.1 * nrm(ks[6], (DEPTH, DIFF_DQK)),
        'lambda_k1': 0.1 * nrm(ks[7], (DEPTH, DIFF_DQK)),
        'lambda_q2': 0.1 * nrm(ks[8], (DEPTH, DIFF_DQK)),
        'lambda_k2': 0.1 * nrm(ks[9], (DEPTH, DIFF_DQK)),
        'diff_norm': gain(ks[10], DIFF_DV),
        'w_o': nrm(ks[11], (DEPTH, MIX_WIDTH, D_MODEL)) * MIX_WIDTH ** -0.5,
        'attn_post_norm': gain(ks[12], D_MODEL),
        'ffn_pre_norm': gain(ks[13], D_MODEL),
        'w_ffn_in': nrm(ks[14], (DEPTH, D_MODEL, 2 * D_FF)) * D_MODEL ** -0.5,
        'conv_w': nrm(ks[15], (DEPTH, CONV_W, D_FF)) * CONV_W ** -0.5,
        'conv_b': 0.02 * nrm(ks[16], (DEPTH, D_FF)),
        'w_ffn_out': nrm(ks[17], (DEPTH, D_FF, D_MODEL)) * D_FF ** -0.5,
        'ffn_post_norm': gain(ks[18], D_MODEL),
    }


def reference(x, attn_pre_norm, w_in, w_alpha_up, b_alpha, gla_norm, lambda_q1, lambda_k1,
              lambda_q2, lambda_k2, diff_norm, w_o, attn_post_norm, ffn_pre_norm, w_ffn_in,
              conv_w, conv_b, w_ffn_out, ffn_post_norm):
    for l in range(DEPTH):
        lambda_init = 0.8 - 0.6 * math.exp(-0.3 * l)
        h = rmsnorm(x, attn_pre_norm[l])
        m = token_mixer(h, w_in[l], w_alpha_up[l], b_alpha[l], gla_norm[l], lambda_q1[l],
                        lambda_k1[l], lambda_q2[l], lambda_k2[l], diff_norm[l], w_o[l], lambda_init)
        x = x + rmsnorm(m, attn_post_norm[l])
        h = rmsnorm(x, ffn_pre_norm[l])
        f = conv_ffn(h, w_ffn_in[l], conv_w[l], conv_b[l], w_ffn_out[l])
        x = x + rmsnorm(f, ffn_post_norm[l])
    return x
```

```python
import functools
import math

import jax
import jax.numpy as jnp
from jax import lax
from jax.experimental import pallas as pl
from jax.experimental.pallas import tpu as pltpu

F32 = jnp.float32
BF16 = jnp.bfloat16

D_MODEL = 2048
GLA_HEADS = 4
GLA_DK = 128
GLA_DV = 256
GLA_RANK = 16
GLA_TAU = 16.0
DIFF_HEADS = 4
DIFF_DQK = 128
DIFF_DV = 256
D_FF = 5632
CONV_W = 3
EPS = 1e-6
LAMBDA_INIT = 0.8 - 0.6 * math.exp(-0.3 * 0)

GLA_QK = GLA_HEADS * GLA_DK
GLA_V = GLA_HEADS * GLA_DV
DIFF_QK = DIFF_HEADS * 2 * DIFF_DQK
DIFF_V = DIFF_HEADS * DIFF_DV

COL_GQ = 0
COL_GK = COL_GQ + GLA_QK
COL_GV = COL_GK + GLA_QK
COL_GG = COL_GV + GLA_V
COL_DQ = COL_GG + GLA_V
COL_DK = COL_DQ + DIFF_QK
COL_DV = COL_DK + DIFF_QK
PROJ_COLS = COL_DV + DIFF_V
GA_OFFSET = GLA_QK + GLA_QK + GLA_V + GLA_V

V7X_LANES = 128
V7X_SUBLANES = 8
V7X_VMEM_LIMIT_BYTES = 56 * 1024 * 1024

GLA_CHUNK = 64
GLA_SUB = 16
NEG_BIG = -1e30


def _rms(v, g):
    return v * lax.rsqrt(jnp.mean(v * v, axis=-1, keepdims=True) + EPS) * g


def _params(dims):
    return pltpu.CompilerParams(dimension_semantics=dims,
                                vmem_limit_bytes=V7X_VMEM_LIMIT_BYTES)


def _in_proj_kernel(x_ref, g_ref, w_ref, wga_ref, cs_ref, o_ref, ga_ref, h_scr):
    @pl.when(pl.program_id(1) == 0)
    def _():
        h = _rms(x_ref[...], g_ref[...]).astype(BF16)
        h_scr[...] = h
        ga_ref[...] = jnp.dot(h, wga_ref[...], preferred_element_type=F32)

    acc = jnp.dot(h_scr[...], w_ref[...], preferred_element_type=F32)
    o_ref[...] = (acc * cs_ref[...]).astype(BF16)


def _in_proj(x, g, w_main, w_ga, colscale):
    S = x.shape[0]
    tm = min(1024, S)
    tn = 1024
    return pl.pallas_call(
        _in_proj_kernel,
        grid=(S // tm, PROJ_COLS // tn),
        in_specs=[
            pl.BlockSpec((tm, D_MODEL), lambda i, j: (i, 0)),
            pl.BlockSpec((1, D_MODEL), lambda i, j: (0, 0)),
            pl.BlockSpec((D_MODEL, tn), lambda i, j: (0, j)),
            pl.BlockSpec((D_MODEL, V7X_LANES), lambda i, j: (0, 0)),
            pl.BlockSpec((1, tn), lambda i, j: (0, j)),
        ],
        out_specs=[
            pl.BlockSpec((tm, tn), lambda i, j: (i, j)),
            pl.BlockSpec((tm, V7X_LANES), lambda i, j: (i, 0)),
        ],
        out_shape=[
            jax.ShapeDtypeStruct((S, PROJ_COLS), BF16),
            jax.ShapeDtypeStruct((S, V7X_LANES), F32),
        ],
        scratch_shapes=[pltpu.VMEM((tm, D_MODEL), BF16)],
        compiler_params=_params(("parallel", "arbitrary")),
        name="in_proj",
    )(x, g, w_main, w_ga, colscale)


def _log_sigmoid(x):
    return jnp.minimum(x, 0.0) - jnp.log(1.0 + jnp.exp(-jnp.abs(x)))


def _gla_kernel(q_ref, k_ref, v_ref, gate_ref, ga_ref, wup_ref, ba_ref, gn_ref,
                o_ref, st_scr):
    C = GLA_CHUNK

    @pl.when(pl.program_id(0) == 0)
    def _():
        st_scr[...] = jnp.zeros_like(st_scr)

    row = lax.broadcasted_iota(jnp.int32, (C, C), 0)
    col = lax.broadcasted_iota(jnp.int32, (C, C), 1)
    tril = jnp.where(col <= row, 1.0, 0.0).astype(BF16)
    sub_row = lax.broadcasted_iota(jnp.int32, (GLA_SUB, C), 0)
    sub_col = lax.broadcasted_iota(jnp.int32, (GLA_SUB, C), 1)
    ga = ga_ref[...].astype(BF16)
    gn = gn_ref[...]

    for h in range(GLA_HEADS):
        ks = slice(h * GLA_DK, (h + 1) * GLA_DK)
        vs = slice(h * GLA_DV, (h + 1) * GLA_DV)
        q = q_ref[:, ks].astype(F32)
        k = k_ref[:, ks].astype(F32)
        v = v_ref[:, vs]

        z = jnp.dot(ga, wup_ref[:, ks], preferred_element_type=F32) + ba_ref[:, ks]
        la = _log_sigmoid(z) * (1.0 / GLA_TAU)
        la_hi = la.astype(BF16)
        la_lo = (la - la_hi.astype(F32)).astype(BF16)
        b = (jnp.dot(tril, la_hi, preferred_element_type=F32)
             + jnp.dot(tril, la_lo, preferred_element_type=F32))

        st = st_scr[h]
        q_in = (q * jnp.exp(b)).astype(BF16)
        o = lax.dot_general(q_in, st.astype(BF16), (((1,), (1,)), ((), ())),
                            preferred_element_type=F32)

        a_rows = []
        for blk in range(C // GLA_SUB):
            r0 = blk * GLA_SUB
            b_blk = b[r0:r0 + GLA_SUB]
            q_blk = q[r0:r0 + GLA_SUB]
            b_ref0 = b[r0:r0 + 1]
            a_blk = jnp.zeros((GLA_SUB, C), F32)
            if blk > 0:
                q_t = (q_blk * jnp.exp(b_blk - b_ref0)).astype(BF16)
                k_t = (k * jnp.exp(jnp.minimum(b_ref0 - b, 0.0))).astype(BF16)
                off = lax.dot_general(q_t, k_t, (((1,), (1,)), ((), ())),
                                      preferred_element_type=F32)
                a_blk = jnp.where(sub_col < r0, off, 0.0)
            for jj in range(GLA_SUB):
                kj = k[r0 + jj:r0 + jj + 1]
                bj = b[r0 + jj:r0 + jj + 1]
                t = q_blk * kj * jnp.exp(jnp.minimum(b_blk - bj, 0.0))
                cj = jnp.sum(t, axis=-1, keepdims=True)
                a_blk = jnp.where((sub_col == r0 + jj) & (sub_row >= jj), cj, a_blk)
            a_rows.append(a_blk)
        a = jnp.concatenate(a_rows, axis=0).astype(BF16)
        o = o + jnp.dot(a, v, preferred_element_type=F32)

        b_last = b[C - 1:C]
        k_out = (k * jnp.exp(b_last - b)).astype(BF16)
        upd = lax.dot_general(v, k_out, (((0,), (0,)), ((), ())),
                              preferred_element_type=F32)
        st_scr[h] = st * jnp.exp(b_last) + upd

        gate = gate_ref[:, vs].astype(F32)
        y = _rms(o, gn) * (gate * jax.nn.sigmoid(gate))
        o_ref[:, vs] = y.astype(BF16)


def _gla(proj, ga, wup, b_alpha, gla_norm):
    S = proj.shape[0]
    C = GLA_CHUNK
    return pl.pallas_call(
        _gla_kernel,
        grid=(S // C,),
        in_specs=[
            pl.BlockSpec((C, GLA_QK), lambda i: (i, COL_GQ // GLA_QK)),
            pl.BlockSpec((C, GLA_QK), lambda i: (i, COL_GK // GLA_QK)),
            pl.BlockSpec((C, GLA_V), lambda i: (i, COL_GV // GLA_V)),
            pl.BlockSpec((C, GLA_V), lambda i: (i, COL_GG // GLA_V)),
            pl.BlockSpec((C, V7X_LANES), lambda i: (i, 0)),
            pl.BlockSpec((V7X_LANES, GLA_QK), lambda i: (0, 0)),
            pl.BlockSpec((1, GLA_QK), lambda i: (0, 0)),
            pl.BlockSpec((1, GLA_DV), lambda i: (0, 0)),
        ],
        out_specs=pl.BlockSpec((C, GLA_V), lambda i: (i, 0)),
        out_shape=jax.ShapeDtypeStruct((S, GLA_V), BF16),
        scratch_shapes=[pltpu.VMEM((GLA_HEADS, GLA_DV, GLA_DK), F32)],
        compiler_params=_params(("arbitrary",)),
        name="gla",
    )(proj, proj, proj, proj, ga, wup, b_alpha, gla_norm)


def _attn_kernel(slopes_ref, lq1_ref, lk1_ref, lq2_ref, lk2_ref, gn_ref,
                 q_ref, k_ref, v_ref, o_ref, m_scr, l_scr, acc_scr, *, T):
    h = pl.program_id(0)
    i = pl.program_id(1)
    slope = slopes_ref[h]
    D = DIFF_DQK

    m_scr[...] = jnp.full_like(m_scr, NEG_BIG)
    l_scr[...] = jnp.zeros_like(l_scr)
    acc_scr[...] = jnp.zeros_like(acc_scr)

    col = lax.broadcasted_iota(jnp.int32, (1, T), 1)

    def step(j, masked):
        r0 = pl.multiple_of(j * T, T)
        v = v_ref[pl.ds(r0, T), :]
        cb = slope * ((j - i) * T + col).astype(F32)
        if masked:
            rr = lax.broadcasted_iota(jnp.int32, (T, T), 0)
            cc = lax.broadcasted_iota(jnp.int32, (T, T), 1)
            keep = cc <= rr
        for c in range(2):
            q = q_ref[:, c * D:(c + 1) * D]
            k = k_ref[pl.ds(r0, T), c * D:(c + 1) * D]
            s = lax.dot_general(q, k, (((1,), (1,)), ((), ())),
                                preferred_element_type=F32) + cb
            if masked:
                s = jnp.where(keep, s, NEG_BIG)
            m_prev = m_scr[c]
            m_new = jnp.maximum(m_prev, jnp.max(s, axis=-1, keepdims=True))
            alpha = jnp.exp(m_prev - m_new)
            p = jnp.exp(s - m_new)
            l_scr[c] = alpha * l_scr[c] + jnp.sum(p, axis=-1, keepdims=True)
            acc_scr[c] = alpha * acc_scr[c] + jnp.dot(
                p.astype(BF16), v, preferred_element_type=F32)
            m_scr[c] = m_new

    def body(j, carry):
        step(j, False)
        return carry

    lax.fori_loop(0, i, body, 0)
    step(i, True)

    lam = (jnp.exp(jnp.sum(lq1_ref[...] * lk1_ref[...], axis=-1, keepdims=True))
           - jnp.exp(jnp.sum(lq2_ref[...] * lk2_ref[...], axis=-1, keepdims=True))
           + LAMBDA_INIT)
    o1 = acc_scr[0] / l_scr[0]
    o2 = acc_scr[1] / l_scr[1]
    o = o1 - lam * o2
    o_ref[...] = (_rms(o, gn_ref[...]) * (1.0 - LAMBDA_INIT)).astype(BF16)


def _diff_attn(proj, slopes, lq1, lk1, lq2, lk2, diff_norm):
    S = proj.shape[0]
    T = min(512, S)
    W = 2 * DIFF_DQK
    vec = pl.BlockSpec((1, DIFF_DQK), lambda h, i: (0, 0))
    return pl.pallas_call(
        functools.partial(_attn_kernel, T=T),
        grid=(DIFF_HEADS, S // T),
        in_specs=[
            pl.BlockSpec(memory_space=pltpu.SMEM),
            vec, vec, vec, vec,
            pl.BlockSpec((1, DIFF_DV), lambda h, i: (0, 0)),
            pl.BlockSpec((T, W), lambda h, i: (i, COL_DQ // W + h)),
            pl.BlockSpec((S, W), lambda h, i: (0, COL_DK // W + h)),
            pl.BlockSpec((S, DIFF_DV), lambda h, i: (0, COL_DV // DIFF_DV + h)),
        ],
        out_specs=pl.BlockSpec((T, DIFF_DV), lambda h, i: (i, h)),
        out_shape=jax.ShapeDtypeStruct((S, DIFF_V), BF16),
        scratch_shapes=[
            pltpu.VMEM((2, T, 1), F32),
            pltpu.VMEM((2, T, 1), F32),
            pltpu.VMEM((2, T, DIFF_DV), F32),
        ],
        compiler_params=_params(("parallel", "arbitrary")),
        name="diff_attn",
    )(slopes, lq1, lk1, lq2, lk2, diff_norm, proj, proj, proj)


def _out_proj_kernel(oa_ref, ob_ref, wo_ref, x_ref, gpost_ref, gpre_ref,
                     x1_ref, h2_ref):
    m = (jnp.dot(oa_ref[...], wo_ref[0:GLA_V, :], preferred_element_type=F32)
         + jnp.dot(ob_ref[...], wo_ref[GLA_V:GLA_V + DIFF_V, :],
                   preferred_element_type=F32))
    x1 = x_ref[...] + _rms(m, gpost_ref[...])
    x1_ref[...] = x1
    h2_ref[...] = _rms(x1, gpre_ref[...]).astype(BF16)


def _out_proj(o_a, o_b, w_o, x, g_post, g_pre):
    S = x.shape[0]
    tm = min(512, S)
    row = lambda i: (i, 0)
    fixed = lambda i: (0, 0)
    return pl.pallas_call(
        _out_proj_kernel,
        grid=(S // tm,),
        in_specs=[
            pl.BlockSpec((tm, GLA_V), row),
            pl.BlockSpec((tm, DIFF_V), row),
            pl.BlockSpec((GLA_V + DIFF_V, D_MODEL), fixed),
            pl.BlockSpec((tm, D_MODEL), row),
            pl.BlockSpec((1, D_MODEL), fixed),
            pl.BlockSpec((1, D_MODEL), fixed),
        ],
        out_specs=[
            pl.BlockSpec((tm, D_MODEL), row),
            pl.BlockSpec((tm, D_MODEL), row),
        ],
        out_shape=[
            jax.ShapeDtypeStruct((S, D_MODEL), F32),
            jax.ShapeDtypeStruct((S, D_MODEL), BF16),
        ],
        compiler_params=_params(("parallel",)),
        name="out_proj",
    )(o_a, o_b, w_o, x, g_post, g_pre)


def _ffn_kernel(h2_ref, wa_ref, wb_ref, cw_ref, cb_ref, wout_ref, x1_ref, gpost_ref,
                o_ref, acc_scr, abuf_scr, halo_scr, *, tm):
    i = pl.program_id(0)
    j = pl.program_id(1)
    nj = pl.num_programs(1)
    P = V7X_SUBLANES

    h2 = h2_ref[...]
    a = jnp.dot(h2, wa_ref[...], preferred_element_type=F32)
    b = jnp.dot(h2, wb_ref[...], preferred_element_type=F32)

    @pl.when(i == 0)
    def _():
        abuf_scr[0:P, :] = jnp.zeros((P, a.shape[1]), F32)

    @pl.when(i > 0)
    def _():
        abuf_scr[0:P, :] = halo_scr[j]

    abuf_scr[P:P + tm, :] = a
    halo_scr[j] = a[tm - P:tm]
    a1 = abuf_scr[P - 1:P - 1 + tm, :]
    a2 = abuf_scr[P - 2:P - 2 + tm, :]
    cw = cw_ref[...]
    conv = cw[0:1] * a2 + cw[1:2] * a1 + cw[2:3] * a + cb_ref[...]
    g = (jax.nn.gelu(conv, approximate=True) * b).astype(BF16)
    contrib = jnp.dot(g, wout_ref[...], preferred_element_type=F32)

    @pl.when(j == 0)
    def _():
        acc_scr[...] = contrib

    @pl.when(j > 0)
    def _():
        acc_scr[...] += contrib

    @pl.when(j == nj - 1)
    def _():
        o_ref[...] = x1_ref[...] + _rms(acc_scr[...], gpost_ref[...])


def _ffn(h2, w_ffn_in, conv_w, conv_b, w_ffn_out, x1, g_post):
    S = h2.shape[0]
    tm = min(512, S)
    tn = 512
    nj = D_FF // tn
    row = lambda i, j: (i, 0)
    return pl.pallas_call(
        functools.partial(_ffn_kernel, tm=tm),
        grid=(S // tm, nj),
        in_specs=[
            pl.BlockSpec((tm, D_MODEL), row),
            pl.BlockSpec((D_MODEL, tn), lambda i, j: (0, j)),
            pl.BlockSpec((D_MODEL, tn), lambda i, j: (0, nj + j)),
            pl.BlockSpec((CONV_W, tn), lambda i, j: (0, j)),
            pl.BlockSpec((1, tn), lambda i, j: (0, j)),
            pl.BlockSpec((tn, D_MODEL), lambda i, j: (j, 0)),
            pl.BlockSpec((tm, D_MODEL), row),
            pl.BlockSpec((1, D_MODEL), lambda i, j: (0, 0)),
        ],
        out_specs=pl.BlockSpec((tm, D_MODEL), row),
        out_shape=jax.ShapeDtypeStruct((S, D_MODEL), F32),
        scratch_shapes=[
            pltpu.VMEM((tm, D_MODEL), F32),
            pltpu.VMEM((tm + V7X_SUBLANES, tn), F32),
            pltpu.VMEM((nj, V7X_SUBLANES, tn), F32),
        ],
        compiler_params=_params(("arbitrary", "arbitrary")),
        name="ffn",
    )(h2, w_ffn_in, w_ffn_in, conv_w, conv_b, w_ffn_out, x1, g_post)


def _layer(x, attn_pre_norm, w_in, w_alpha_up, b_alpha, gla_norm, lambda_q1, lambda_k1,
           lambda_q2, lambda_k2, diff_norm, w_o, attn_post_norm, ffn_pre_norm, w_ffn_in,
           conv_w, conv_b, w_ffn_out, ffn_post_norm):
    vec = lambda p: p.reshape(1, -1).astype(F32)

    w_main = jnp.concatenate(
        [w_in[:, :GA_OFFSET], w_in[:, GA_OFFSET + GLA_RANK:]], axis=1).astype(BF16)
    w_ga = jnp.pad(w_in[:, GA_OFFSET:GA_OFFSET + GLA_RANK],
                   ((0, 0), (0, V7X_LANES - GLA_RANK))).astype(BF16)
    colscale = jnp.concatenate([
        jnp.full((GLA_QK,), GLA_DK ** -0.5, F32),
        jnp.ones((COL_DQ - COL_GK,), F32),
        jnp.full((DIFF_QK,), DIFF_DQK ** -0.5, F32),
        jnp.ones((PROJ_COLS - COL_DK,), F32),
    ]).reshape(1, PROJ_COLS)
    wup = jnp.pad(w_alpha_up, ((0, V7X_LANES - GLA_RANK), (0, 0))).astype(BF16)
    slopes = jnp.asarray(
        [2.0 ** (-8.0 * (h + 1) / DIFF_HEADS) for h in range(DIFF_HEADS)], F32)

    proj, ga = _in_proj(x, vec(attn_pre_norm), w_main, w_ga, colscale)
    o_a = _gla(proj, ga, wup, vec(b_alpha), vec(gla_norm))
    o_b = _diff_attn(proj, slopes, vec(lambda_q1), vec(lambda_k1), vec(lambda_q2),
                     vec(lambda_k2), vec(diff_norm))
    x1, h2 = _out_proj(o_a, o_b, w_o.astype(BF16), x, vec(attn_post_norm),
                       vec(ffn_pre_norm))
    return _ffn(h2, w_ffn_in.astype(BF16), conv_w.astype(F32), vec(conv_b),
                w_ffn_out.astype(BF16), x1, vec(ffn_post_norm))


def kernel(x, attn_pre_norm, w_in, w_alpha_up, b_alpha, gla_norm, lambda_q1, lambda_k1,
           lambda_q2, lambda_k2, diff_norm, w_o, attn_post_norm, ffn_pre_norm, w_ffn_in,
           conv_w, conv_b, w_ffn_out, ffn_post_norm):
    B = x.shape[0]
    depth = w_in.shape[0]
    assert depth == 1, "lambda_init is baked for a single layer"
    outs = []
    for bi in range(B):
        xb = x[bi]
        for l in range(depth):
            xb = _layer(xb, attn_pre_norm[l], w_in[l], w_alpha_up[l], b_alpha[l], gla_norm[l],
                        lambda_q1[l], lambda_k1[l], lambda_q2[l], lambda_k2[l], diff_norm[l],
                        w_o[l], attn_post_norm[l], ffn_pre_norm[l], w_ffn_in[l], conv_w[l],
                        conv_b[l], w_ffn_out[l], ffn_post_norm[l])
        outs.append(xb)
    return outs[0][None] if B == 1 else jnp.stack(outs, axis=0)
```

```python
import functools
import math

import jax
import jax.numpy as jnp
from jax import lax
from jax.experimental import pallas as pl
from jax.experimental.pallas import tpu as pltpu

F32 = jnp.float32
BF16 = jnp.bfloat16

D_MODEL = 2048
GLA_HEADS = 4
GLA_DK = 128
GLA_DV = 256
GLA_RANK = 16
GLA_TAU = 16.0
DIFF_HEADS = 4
DIFF_DQK = 128
DIFF_DV = 256
D_FF = 5632
CONV_W = 3
EPS = 1e-6
LAMBDA_INIT = 0.8 - 0.6 * math.exp(-0.3 * 0)
LOG2E = math.log2(math.e)

GLA_QK = GLA_HEADS * GLA_DK
GLA_V = GLA_HEADS * GLA_DV
DIFF_QK = DIFF_HEADS * 2 * DIFF_DQK
DIFF_V = DIFF_HEADS * DIFF_DV

COL_GQ = 0
COL_GK = COL_GQ + GLA_QK
COL_GV = COL_GK + GLA_QK
COL_GG = COL_GV + GLA_V
COL_DQ = COL_GG + GLA_V
COL_DK = COL_DQ + DIFF_QK
COL_DV = COL_DK + DIFF_QK
PROJ_COLS = COL_DV + DIFF_V
GA_OFFSET = GLA_QK + GLA_QK + GLA_V + GLA_V

V7X_LANES = 128
V7X_SUBLANES = 8
V7X_VMEM_LIMIT_BYTES = 56 * 1024 * 1024

ATT_T = 512
ATT_TQ = 1024
ATT_QW = 256
GLA_CHUNK = 64
GLA_SUB = 16
NEG_BIG = -1e30


def _rms(v, g):
    return v * lax.rsqrt(jnp.mean(v * v, axis=-1, keepdims=True) + EPS) * g


def _params(dims):
    return pltpu.CompilerParams(dimension_semantics=dims,
                                vmem_limit_bytes=V7X_VMEM_LIMIT_BYTES)


def _in_proj_kernel(x_ref, g_ref, w_ref, wga_ref, cs_ref, o_ref, ga_ref, vt_ref, h_scr,
                    *, tn):
    j = pl.program_id(1)

    @pl.when(j == 0)
    def _():
        h = _rms(x_ref[...], g_ref[...]).astype(BF16)
        h_scr[...] = h
        ga_ref[...] = jnp.dot(h, wga_ref[...], preferred_element_type=F32)

    acc = jnp.dot(h_scr[...], w_ref[...], preferred_element_type=F32)
    o_ref[...] = (acc * cs_ref[...]).astype(BF16)

    @pl.when(j == COL_DV // tn)
    def _():
        acc_t = acc.T.astype(BF16)
        for kb in range(vt_ref.shape[0]):
            vt_ref[kb] = acc_t[:, kb * ATT_T:(kb + 1) * ATT_T]


def _in_proj(x, g, w_main, w_ga, colscale):
    S = x.shape[0]
    tm = min(1024, S)
    tn = DIFF_V
    return pl.pallas_call(
        functools.partial(_in_proj_kernel, tn=tn),
        grid=(S // tm, PROJ_COLS // tn),
        in_specs=[
            pl.BlockSpec((tm, D_MODEL), lambda i, j: (i, 0)),
            pl.BlockSpec((1, D_MODEL), lambda i, j: (0, 0)),
            pl.BlockSpec((D_MODEL, tn), lambda i, j: (0, j)),
            pl.BlockSpec((D_MODEL, V7X_LANES), lambda i, j: (0, 0)),
            pl.BlockSpec((1, tn), lambda i, j: (0, j)),
        ],
        out_specs=[
            pl.BlockSpec((tm, tn), lambda i, j: (i, j)),
            pl.BlockSpec((tm, V7X_LANES), lambda i, j: (i, 0)),
            pl.BlockSpec((tm // ATT_T, DIFF_V, ATT_T), lambda i, j: (i, 0, 0)),
        ],
        out_shape=[
            jax.ShapeDtypeStruct((S, PROJ_COLS), BF16),
            jax.ShapeDtypeStruct((S, V7X_LANES), F32),
            jax.ShapeDtypeStruct((S // ATT_T, DIFF_V, ATT_T), BF16),
        ],
        scratch_shapes=[pltpu.VMEM((tm, D_MODEL), BF16)],
        compiler_params=_params(("parallel", "arbitrary")),
        name="in_proj",
    )(x, g, w_main, w_ga, colscale)


def _log_sigmoid(x):
    return jnp.minimum(x, 0.0) - jnp.log(1.0 + jnp.exp(-jnp.abs(x)))


def _gla_kernel(q_ref, k_ref, v_ref, gate_ref, ga_ref, wup_ref, ba_ref, gn_ref,
                o_ref, st_scr):
    C = GLA_CHUNK

    @pl.when(pl.program_id(0) == 0)
    def _():
        st_scr[...] = jnp.zeros_like(st_scr)

    row = lax.broadcasted_iota(jnp.int32, (C, C), 0)
    col = lax.broadcasted_iota(jnp.int32, (C, C), 1)
    tril = jnp.where(col <= row, 1.0, 0.0).astype(BF16)
    sub_row = lax.broadcasted_iota(jnp.int32, (GLA_SUB, C), 0)
    sub_col = lax.broadcasted_iota(jnp.int32, (GLA_SUB, C), 1)
    ga = ga_ref[...].astype(BF16)
    gn = gn_ref[...]

    for h in range(GLA_HEADS):
        ks = slice(h * GLA_DK, (h + 1) * GLA_DK)
        vs = slice(h * GLA_DV, (h + 1) * GLA_DV)
        q = q_ref[:, ks].astype(F32)
        k = k_ref[:, ks].astype(F32)
        v = v_ref[:, vs]

        z = jnp.dot(ga, wup_ref[:, ks], preferred_element_type=F32) + ba_ref[:, ks]
        la = _log_sigmoid(z) * (1.0 / GLA_TAU)
        la_hi = la.astype(BF16)
        la_lo = (la - la_hi.astype(F32)).astype(BF16)
        b = (jnp.dot(tril, la_hi, preferred_element_type=F32)
             + jnp.dot(tril, la_lo, preferred_element_type=F32))

        st = st_scr[h]
        q_in = (q * jnp.exp(b)).astype(BF16)
        o = lax.dot_general(q_in, st.astype(BF16), (((1,), (1,)), ((), ())),
                            preferred_element_type=F32)

        a_rows = []
        for blk in range(C // GLA_SUB):
            r0 = blk * GLA_SUB
            b_blk = b[r0:r0 + GLA_SUB]
            q_blk = q[r0:r0 + GLA_SUB]
            b_ref0 = b[r0:r0 + 1]
            a_blk = jnp.zeros((GLA_SUB, C), F32)
            if blk > 0:
                q_t = (q_blk * jnp.exp(b_blk - b_ref0)).astype(BF16)
                k_t = (k * jnp.exp(jnp.minimum(b_ref0 - b, 0.0))).astype(BF16)
                off = lax.dot_general(q_t, k_t, (((1,), (1,)), ((), ())),
                                      preferred_element_type=F32)
                a_blk = jnp.where(sub_col < r0, off, 0.0)
            for jj in range(GLA_SUB):
                kj = k[r0 + jj:r0 + jj + 1]
                bj = b[r0 + jj:r0 + jj + 1]
                t = q_blk * kj * jnp.exp(jnp.minimum(b_blk - bj, 0.0))
                cj = jnp.sum(t, axis=-1, keepdims=True)
                a_blk = jnp.where((sub_col == r0 + jj) & (sub_row >= jj), cj, a_blk)
            a_rows.append(a_blk)
        a = jnp.concatenate(a_rows, axis=0).astype(BF16)
        o = o + jnp.dot(a, v, preferred_element_type=F32)

        b_last = b[C - 1:C]
        k_out = (k * jnp.exp(b_last - b)).astype(BF16)
        upd = lax.dot_general(v, k_out, (((0,), (0,)), ((), ())),
                              preferred_element_type=F32)
        st_scr[h] = st * jnp.exp(b_last) + upd

        gate = gate_ref[:, vs].astype(F32)
        y = _rms(o, gn) * (gate * jax.nn.sigmoid(gate))
        o_ref[:, vs] = y.astype(BF16)


def _gla(proj, ga, wup, b_alpha, gla_norm):
    S = proj.shape[0]
    C = GLA_CHUNK
    return pl.pallas_call(
        _gla_kernel,
        grid=(S // C,),
        in_specs=[
            pl.BlockSpec((C, GLA_QK), lambda i: (i, COL_GQ // GLA_QK)),
            pl.BlockSpec((C, GLA_QK), lambda i: (i, COL_GK // GLA_QK)),
            pl.BlockSpec((C, GLA_V), lambda i: (i, COL_GV // GLA_V)),
            pl.BlockSpec((C, GLA_V), lambda i: (i, COL_GG // GLA_V)),
            pl.BlockSpec((C, V7X_LANES), lambda i: (i, 0)),
            pl.BlockSpec((V7X_LANES, GLA_QK), lambda i: (0, 0)),
            pl.BlockSpec((1, GLA_QK), lambda i: (0, 0)),
            pl.BlockSpec((1, GLA_DV), lambda i: (0, 0)),
        ],
        out_specs=pl.BlockSpec((C, GLA_V), lambda i: (i, 0)),
        out_shape=jax.ShapeDtypeStruct((S, GLA_V), BF16),
        scratch_shapes=[pltpu.VMEM((GLA_HEADS, GLA_DV, GLA_DK), F32)],
        compiler_params=_params(("arbitrary",)),
        name="gla",
    )(proj, proj, proj, proj, ga, wup, b_alpha, gla_norm)


def _attn_kernel(slopes_ref, lq1_ref, lk1_ref, lq2_ref, lk2_ref, gn_ref,
                 q_ref, k_ref, vt_ref, o_ref, feat_scr, qa_scr, s_scr, m_scr, l_scr, acc_scr):
    TQ, TK, QW, D = q_ref.shape[0], ATT_T, ATT_QW, DIFF_DQK
    h = pl.program_id(0)
    i = pl.program_id(1)
    n_kb = TQ // TK
    lane = lax.broadcasted_iota(jnp.int32, (TK, V7X_LANES), 1)

    @pl.when(i == 0)
    def _():
        slope2 = slopes_ref[h] * LOG2E
        row = lax.broadcasted_iota(jnp.int32, (TK, V7X_LANES), 0)

        def fill(blk, carry):
            r0 = pl.multiple_of(blk * TK, TK)
            b = slope2 * (row + r0).astype(F32)
            hi = b.astype(BF16).astype(F32)
            r1 = b - hi
            mid = r1.astype(BF16).astype(F32)
            lo = r1 - mid
            feat = jnp.where(lane == 0, hi,
                             jnp.where(lane == 1, mid, jnp.where(lane == 2, lo, 0.0)))
            feat_scr[pl.ds(r0, TK), :] = feat.astype(BF16)
            return carry

        lax.fori_loop(0, feat_scr.shape[0] // TK, fill, 0)

    m_scr[...] = jnp.full_like(m_scr, NEG_BIG)
    l_scr[...] = jnp.zeros_like(l_scr)
    acc_scr[...] = jnp.zeros_like(acc_scr)

    q_lane = lax.broadcasted_iota(jnp.int32, (TQ, V7X_LANES), 1)
    q_ones = jnp.where(q_lane < 3, 1.0, 0.0).astype(BF16)
    for c in range(2):
        qa_scr[c, :, 0:D] = q_ref[:, c * D:(c + 1) * D]
        qa_scr[c, :, D:2 * D] = q_ones
    key_row = lax.broadcasted_iota(jnp.int32, (TK, QW), 0)
    query_col = lax.broadcasted_iota(jnp.int32, (TK, QW), 1)

    def slabs(key_off):
        return [sl for sl in range(TQ // QW)
                if key_off is None or key_off <= sl * QW + QW - 1]

    def scores(j, slot, key_off):
        r0 = pl.multiple_of(j * TK, TK)
        feat = feat_scr[pl.ds(r0, TK), :]
        for c in range(2):
            k = jnp.concatenate([k_ref[pl.ds(r0, TK), c * D:(c + 1) * D], feat], axis=1)
            for sl in slabs(key_off):
                qs = slice(sl * QW, (sl + 1) * QW)
                s_scr[slot, c, :, qs] = lax.dot_general(
                    k, qa_scr[c, qs, :], (((1,), (1,)), ((), ())),
                    preferred_element_type=F32)

    def accumulate(j, slot, key_off):
        vt = vt_ref[j]
        for c in range(2):
            for sl in slabs(key_off):
                q_lo = sl * QW
                qs = slice(q_lo, q_lo + QW)
                s = s_scr[slot, c, :, qs]
                if key_off is not None and key_off + TK - 1 > q_lo:
                    s = jnp.where(key_row + key_off <= query_col + q_lo, s, NEG_BIG)
                m_prev = m_scr[c, :, qs]
                m_new = jnp.maximum(m_prev, jnp.max(s, axis=0, keepdims=True))
                alpha = jnp.exp2(m_prev - m_new)
                p = jnp.exp2(s - m_new)
                l_scr[c, :, qs] = alpha * l_scr[c, :, qs] + jnp.sum(p, axis=0, keepdims=True)
                acc_scr[c, :, qs] = alpha * acc_scr[c, :, qs] + jnp.dot(
                    vt, p.astype(BF16), preferred_element_type=F32)
                m_scr[c, :, qs] = m_new

    assert n_kb == 2
    scores(0, 0, None)

    def body(jj, carry):
        j0 = 2 * jj
        scores(j0 + 1, 1, None)
        accumulate(j0, 0, None)
        scores(j0 + 2, 0, None)
        accumulate(j0 + 1, 1, None)
        return carry

    lax.fori_loop(0, i, body, 0)
    scores(2 * i + 1, 1, TK)
    accumulate(2 * i, 0, 0)
    accumulate(2 * i + 1, 1, TK)

    lam = (jnp.exp(jnp.sum(lq1_ref[...] * lk1_ref[...], axis=-1, keepdims=True))
           - jnp.exp(jnp.sum(lq2_ref[...] * lk2_ref[...], axis=-1, keepdims=True))
           + LAMBDA_INIT)
    o1 = acc_scr[0] / l_scr[0]
    o2 = acc_scr[1] / l_scr[1]
    o = (o1 - lam * o2).T
    o_ref[...] = (_rms(o, gn_ref[...]) * (1.0 - LAMBDA_INIT)).astype(BF16)


def _diff_attn(proj, vt, slopes, lq1, lk1, lq2, lk2, diff_norm):
    S = proj.shape[0]
    TQ = min(ATT_TQ, S)
    W = 2 * DIFF_DQK
    vec = pl.BlockSpec((1, DIFF_DQK), lambda h, i: (0, 0))
    return pl.pallas_call(
        _attn_kernel,
        grid=(DIFF_HEADS, S // TQ),
        in_specs=[
            pl.BlockSpec(memory_space=pltpu.SMEM),
            vec, vec, vec, vec,
            pl.BlockSpec((1, DIFF_DV), lambda h, i: (0, 0)),
            pl.BlockSpec((TQ, W), lambda h, i: (i, COL_DQ // W + h)),
            pl.BlockSpec((S, W), lambda h, i: (0, COL_DK // W + h)),
            pl.BlockSpec((S // ATT_T, DIFF_DV, ATT_T), lambda h, i: (0, h, 0)),
        ],
        out_specs=pl.BlockSpec((TQ, DIFF_DV), lambda h, i: (i, h)),
        out_shape=jax.ShapeDtypeStruct((S, DIFF_V), BF16),
        scratch_shapes=[
            pltpu.VMEM((S, V7X_LANES), BF16),
            pltpu.VMEM((2, TQ, W), BF16),
            pltpu.VMEM((2, 2, ATT_T, TQ), F32),
            pltpu.VMEM((2, 1, TQ), F32),
            pltpu.VMEM((2, 1, TQ), F32),
            pltpu.VMEM((2, DIFF_DV, TQ), F32),
        ],
        compiler_params=_params(("arbitrary", "arbitrary")),
        name="diff_attn",
    )(slopes, lq1, lk1, lq2, lk2, diff_norm, proj, proj, vt)


def _out_proj_kernel(oa_ref, ob_ref, wo_ref, x_ref, gpost_ref, gpre_ref,
                     x1_ref, h2_ref):
    m = (jnp.dot(oa_ref[...], wo_ref[0:GLA_V, :], preferred_element_type=F32)
         + jnp.dot(ob_ref[...], wo_ref[GLA_V:GLA_V + DIFF_V, :],
                   preferred_element_type=F32))
    x1 = x_ref[...] + _rms(m, gpost_ref[...])
    x1_ref[...] = x1
    h2_ref[...] = _rms(x1, gpre_ref[...]).astype(BF16)


def _out_proj(o_a, o_b, w_o, x, g_post, g_pre):
    S = x.shape[0]
    tm = min(512, S)
    row = lambda i: (i, 0)
    fixed = lambda i: (0, 0)
    return pl.pallas_call(
        _out_proj_kernel,
        grid=(S // tm,),
        in_specs=[
            pl.BlockSpec((tm, GLA_V), row),
            pl.BlockSpec((tm, DIFF_V), row),
            pl.BlockSpec((GLA_V + DIFF_V, D_MODEL), fixed),
            pl.BlockSpec((tm, D_MODEL), row),
            pl.BlockSpec((1, D_MODEL), fixed),
            pl.BlockSpec((1, D_MODEL), fixed),
        ],
        out_specs=[
            pl.BlockSpec((tm, D_MODEL), row),
            pl.BlockSpec((tm, D_MODEL), row),
        ],
        out_shape=[
            jax.ShapeDtypeStruct((S, D_MODEL), F32),
            jax.ShapeDtypeStruct((S, D_MODEL), BF16),
        ],
        compiler_params=_params(("parallel",)),
        name="out_proj",
    )(o_a, o_b, w_o, x, g_post, g_pre)


def _ffn_kernel(h2_ref, wa_ref, wb_ref, cw_ref, cb_ref, wout_ref, x1_ref, gpost_ref,
                o_ref, acc_scr, abuf_scr, halo_scr, *, tm):
    i = pl.program_id(0)
    j = pl.program_id(1)
    nj = pl.num_programs(1)
    P = V7X_SUBLANES

    h2 = h2_ref[...]
    a = jnp.dot(h2, wa_ref[...], preferred_element_type=F32)
    b = jnp.dot(h2, wb_ref[...], preferred_element_type=F32)

    @pl.when(i == 0)
    def _():
        abuf_scr[0:P, :] = jnp.zeros((P, a.shape[1]), F32)

    @pl.when(i > 0)
    def _():
        abuf_scr[0:P, :] = halo_scr[j]

    abuf_scr[P:P + tm, :] = a
    halo_scr[j] = a[tm - P:tm]
    a1 = abuf_scr[P - 1:P - 1 + tm, :]
    a2 = abuf_scr[P - 2:P - 2 + tm, :]
    cw = cw_ref[...]
    conv = cw[0:1] * a2 + cw[1:2] * a1 + cw[2:3] * a + cb_ref[...]
    g = (jax.nn.gelu(conv, approximate=True) * b).astype(BF16)
    contrib = jnp.dot(g, wout_ref[...], preferred_element_type=F32)

    @pl.when(j == 0)
    def _():
        acc_scr[...] = contrib

    @pl.when(j > 0)
    def _():
        acc_scr[...] += contrib

    @pl.when(j == nj - 1)
    def _():
        o_ref[...] = x1_ref[...] + _rms(acc_scr[...], gpost_ref[...])


def _ffn(h2, w_ffn_in, conv_w, conv_b, w_ffn_out, x1, g_post):
    S = h2.shape[0]
    tm = min(512, S)
    tn = 512
    nj = D_FF // tn
    row = lambda i, j: (i, 0)
    return pl.pallas_call(
        functools.partial(_ffn_kernel, tm=tm),
        grid=(S // tm, nj),
        in_specs=[
            pl.BlockSpec((tm, D_MODEL), row),
            pl.BlockSpec((D_MODEL, tn), lambda i, j: (0, j)),
            pl.BlockSpec((D_MODEL, tn), lambda i, j: (0, nj + j)),
            pl.BlockSpec((CONV_W, tn), lambda i, j: (0, j)),
            pl.BlockSpec((1, tn), lambda i, j: (0, j)),
            pl.BlockSpec((tn, D_MODEL), lambda i, j: (j, 0)),
            pl.BlockSpec((tm, D_MODEL), row),
            pl.BlockSpec((1, D_MODEL), lambda i, j: (0, 0)),
        ],
        out_specs=pl.BlockSpec((tm, D_MODEL), row),
        out_shape=jax.ShapeDtypeStruct((S, D_MODEL), F32),
        scratch_shapes=[
            pltpu.VMEM((tm, D_MODEL), F32),
            pltpu.VMEM((tm + V7X_SUBLANES, tn), F32),
            pltpu.VMEM((nj, V7X_SUBLANES, tn), F32),
        ],
        compiler_params=_params(("arbitrary", "arbitrary")),
        name="ffn",
    )(h2, w_ffn_in, w_ffn_in, conv_w, conv_b, w_ffn_out, x1, g_post)


def _layer(x, attn_pre_norm, w_in, w_alpha_up, b_alpha, gla_norm, lambda_q1, lambda_k1,
           lambda_q2, lambda_k2, diff_norm, w_o, attn_post_norm, ffn_pre_norm, w_ffn_in,
           conv_w, conv_b, w_ffn_out, ffn_post_norm):
    vec = lambda p: p.reshape(1, -1).astype(F32)

    w_main = jnp.concatenate(
        [w_in[:, :GA_OFFSET], w_in[:, GA_OFFSET + GLA_RANK:]], axis=1).astype(BF16)
    w_ga = jnp.pad(w_in[:, GA_OFFSET:GA_OFFSET + GLA_RANK],
                   ((0, 0), (0, V7X_LANES - GLA_RANK))).astype(BF16)
    colscale = jnp.concatenate([
        jnp.full((GLA_QK,), GLA_DK ** -0.5, F32),
        jnp.ones((COL_DQ - COL_GK,), F32),
        jnp.full((DIFF_QK,), DIFF_DQK ** -0.5 * LOG2E, F32),
        jnp.ones((PROJ_COLS - COL_DK,), F32),
    ]).reshape(1, PROJ_COLS)
    wup = jnp.pad(w_alpha_up, ((0, V7X_LANES - GLA_RANK), (0, 0))).astype(BF16)
    slopes = jnp.asarray(
        [2.0 ** (-8.0 * (h + 1) / DIFF_HEADS) for h in range(DIFF_HEADS)], F32)

    proj, ga, vt = _in_proj(x, vec(attn_pre_norm), w_main, w_ga, colscale)
    o_a = _gla(proj, ga, wup, vec(b_alpha), vec(gla_norm))
    o_b = _diff_attn(proj, vt, slopes, vec(lambda_q1), vec(lambda_k1), vec(lambda_q2),
                     vec(lambda_k2), vec(diff_norm))
    x1, h2 = _out_proj(o_a, o_b, w_o.astype(BF16), x, vec(attn_post_norm),
                       vec(ffn_pre_norm))
    return _ffn(h2, w_ffn_in.astype(BF16), conv_w.astype(F32), vec(conv_b),
                w_ffn_out.astype(BF16), x1, vec(ffn_post_norm))


def kernel(x, attn_pre_norm, w_in, w_alpha_up, b_alpha, gla_norm, lambda_q1, lambda_k1,
           lambda_q2, lambda_k2, diff_norm, w_o, attn_post_norm, ffn_pre_norm, w_ffn_in,
           conv_w, conv_b, w_ffn_out, ffn_post_norm):
    B = x.shape[0]
    depth = w_in.shape[0]
    assert depth == 1, "lambda_init is baked for a single layer"
    outs = []
    for bi in range(B):
        xb = x[bi]
        for l in range(depth):
            xb = _layer(xb, attn_pre_norm[l], w_in[l], w_alpha_up[l], b_alpha[l], gla_norm[l],
                        lambda_q1[l], lambda_k1[l], lambda_q2[l], lambda_k2[l], diff_norm[l],
                        w_o[l], attn_post_norm[l], ffn_pre_norm[l], w_ffn_in[l], conv_w[l],
                        conv_b[l], w_ffn_out[l], ffn_post_norm[l])
        outs.append(xb)
    return outs[0][None] if B == 1 else jnp.stack(outs, axis=0)
```

```python
import functools
import math

import jax
import jax.numpy as jnp
from jax import lax
from jax.experimental import pallas as pl
from jax.experimental.pallas import tpu as pltpu

F32 = jnp.float32
BF16 = jnp.bfloat16

D_MODEL = 2048
GLA_HEADS = 4
GLA_DK = 128
GLA_DV = 256
GLA_RANK = 16
GLA_TAU = 16.0
DIFF_HEADS = 4
DIFF_DQK = 128
DIFF_DV = 256
D_FF = 5632
CONV_W = 3
EPS = 1e-6
LAMBDA_INIT = 0.8 - 0.6 * math.exp(-0.3 * 0)
LOG2E = math.log2(math.e)

GLA_QK = GLA_HEADS * GLA_DK
GLA_V = GLA_HEADS * GLA_DV
DIFF_QK = DIFF_HEADS * 2 * DIFF_DQK
DIFF_V = DIFF_HEADS * DIFF_DV

COL_GQ = 0
COL_GK = COL_GQ + GLA_QK
COL_GV = COL_GK + GLA_QK
COL_GG = COL_GV + GLA_V
COL_DQ = COL_GG + GLA_V
COL_DK = COL_DQ + DIFF_QK
COL_DV = COL_DK + DIFF_QK
PROJ_COLS = COL_DV + DIFF_V
GA_OFFSET = GLA_QK + GLA_QK + GLA_V + GLA_V

V7X_LANES = 128
V7X_SUBLANES = 8
V7X_VMEM_LIMIT_BYTES = 56 * 1024 * 1024

FFN_CW = 256
FFN_RH = 256
ATT_T = 512
ATT_TQ = 1024
ATT_QW = 256
GLA_CHUNK = 64
GLA_ROWS = 256
GLA_SUB = 16
GLA_SAFE_SPAN = 60.0
NEG_BIG = -1e30


def _rms(v, g):
    return v * lax.rsqrt(jnp.mean(v * v, axis=-1, keepdims=True) + EPS) * g


def _params(dims):
    return pltpu.CompilerParams(dimension_semantics=dims,
                                vmem_limit_bytes=V7X_VMEM_LIMIT_BYTES)


def _in_proj_kernel(x_ref, g_ref, w_ref, wga_ref, cs_ref, o_ref, ga_ref, vt_ref, h_scr,
                    *, tn):
    j = pl.program_id(1)

    @pl.when(j == 0)
    def _():
        h = _rms(x_ref[...], g_ref[...]).astype(BF16)
        h_scr[...] = h
        ga_ref[...] = jnp.dot(h, wga_ref[...], preferred_element_type=F32)

    acc = jnp.dot(h_scr[...], w_ref[...], preferred_element_type=F32)
    o_ref[...] = (acc * cs_ref[...]).astype(BF16)

    @pl.when(j == COL_DV // tn)
    def _():
        acc_t = acc.T.astype(BF16)
        for kb in range(vt_ref.shape[0]):
            vt_ref[kb] = acc_t[:, kb * ATT_T:(kb + 1) * ATT_T]


def _in_proj(x, g, w_main, w_ga, colscale):
    S = x.shape[0]
    tm = min(1024, S)
    tn = DIFF_V
    return pl.pallas_call(
        functools.partial(_in_proj_kernel, tn=tn),
        grid=(S // tm, PROJ_COLS // tn),
        in_specs=[
            pl.BlockSpec((tm, D_MODEL), lambda i, j: (i, 0)),
            pl.BlockSpec((1, D_MODEL), lambda i, j: (0, 0)),
            pl.BlockSpec((D_MODEL, tn), lambda i, j: (0, j)),
            pl.BlockSpec((D_MODEL, V7X_LANES), lambda i, j: (0, 0)),
            pl.BlockSpec((1, tn), lambda i, j: (0, j)),
        ],
        out_specs=[
            pl.BlockSpec((tm, tn), lambda i, j: (i, j)),
            pl.BlockSpec((tm, V7X_LANES), lambda i, j: (i, 0)),
            pl.BlockSpec((tm // ATT_T, DIFF_V, ATT_T), lambda i, j: (i, 0, 0)),
        ],
        out_shape=[
            jax.ShapeDtypeStruct((S, PROJ_COLS), BF16),
            jax.ShapeDtypeStruct((S, V7X_LANES), F32),
            jax.ShapeDtypeStruct((S // ATT_T, DIFF_V, ATT_T), BF16),
        ],
        scratch_shapes=[pltpu.VMEM((tm, D_MODEL), BF16)],
        compiler_params=_params(("parallel", "arbitrary")),
        name="in_proj",
    )(x, g, w_main, w_ga, colscale)


def _log_sigmoid(x):
    return jnp.minimum(x, 0.0) - jnp.log(1.0 + jnp.exp(-jnp.abs(x)))


def _gla_kernel(q_ref, k_ref, v_ref, gate_ref, ga_ref, wup_ref, ba_ref, gn_ref,
                o_ref, st_scr, b_scr, a_scr):
    C, R = GLA_CHUNK, GLA_ROWS
    n_chunks = R // C

    @pl.when(pl.program_id(0) == 0)
    def _():
        st_scr[...] = jnp.zeros_like(st_scr)

    row = lax.broadcasted_iota(jnp.int32, (R, R), 0)
    col = lax.broadcasted_iota(jnp.int32, (R, R), 1)
    shift = C.bit_length() - 1
    in_chunk = (col <= row) & ((row >> shift) == (col >> shift))
    tril = jnp.where(in_chunk, 1.0, 0.0).astype(BF16)
    gn = gn_ref[...]

    z = jnp.dot(ga_ref[...].astype(BF16), wup_ref[...], preferred_element_type=F32) + ba_ref[...]
    la = _log_sigmoid(z) * (1.0 / GLA_TAU)
    la_hi = la.astype(BF16)
    la_lo = (la - la_hi.astype(F32)).astype(BF16)
    b_all = (jnp.dot(tril, la_hi, preferred_element_type=F32)
             + jnp.dot(tril, la_lo, preferred_element_type=F32))
    b_scr[...] = b_all
    b_first = jnp.concatenate(
        [jnp.broadcast_to(b_all[c * C:c * C + 1], (C, GLA_QK)) for c in range(n_chunks)], axis=0)
    span = jnp.max(b_first - b_all)
    small_span = span <= GLA_SAFE_SPAN

    def heads():
        for h in range(GLA_HEADS):
            ks = slice(h * GLA_DK, (h + 1) * GLA_DK)
            yield h, ks, q_ref[:, ks].astype(F32), k_ref[:, ks].astype(F32)

    @pl.when(small_span)
    def _():
        for h, ks, q, k in heads():
            d = b_first[:, ks] - b_scr[:, ks]
            q_t = (q * jnp.exp(-d)).astype(BF16)
            k_t = (k * jnp.exp(d)).astype(BF16)
            s = lax.dot_general(q_t, k_t, (((1,), (1,)), ((), ())),
                                preferred_element_type=F32)
            a_scr[h] = jnp.where(in_chunk, s, 0.0)

    @pl.when(jnp.logical_not(small_span))
    def _():
        a_scr[...] = jnp.zeros_like(a_scr)
        sub_row = lax.broadcasted_iota(jnp.int32, (GLA_SUB, C), 0)
        sub_col = lax.broadcasted_iota(jnp.int32, (GLA_SUB, C), 1)
        for h, ks, q_all, k_all in heads():
            for c in range(n_chunks):
                q = q_all[c * C:(c + 1) * C]
                k = k_all[c * C:(c + 1) * C]
                b = b_scr[c * C:(c + 1) * C, ks]
                for blk in range(C // GLA_SUB):
                    r0 = blk * GLA_SUB
                    b_blk = b[r0:r0 + GLA_SUB]
                    q_blk = q[r0:r0 + GLA_SUB]
                    b_ref0 = b[r0:r0 + 1]
                    a_blk = jnp.zeros((GLA_SUB, C), F32)
                    if blk > 0:
                        q_t = (q_blk * jnp.exp(b_blk - b_ref0)).astype(BF16)
                        k_t = (k * jnp.exp(jnp.minimum(b_ref0 - b, 0.0))).astype(BF16)
                        off = lax.dot_general(q_t, k_t, (((1,), (1,)), ((), ())),
                                              preferred_element_type=F32)
                        a_blk = jnp.where(sub_col < r0, off, 0.0)
                    for jj in range(GLA_SUB):
                        kj = k[r0 + jj:r0 + jj + 1]
                        bj = b[r0 + jj:r0 + jj + 1]
                        t = q_blk * kj * jnp.exp(jnp.minimum(b_blk - bj, 0.0))
                        cj = jnp.sum(t, axis=-1, keepdims=True)
                        a_blk = jnp.where((sub_col == r0 + jj) & (sub_row >= jj), cj, a_blk)
                    a_scr[h, c * C + r0:c * C + r0 + GLA_SUB, c * C:(c + 1) * C] = a_blk

    for h, ks, q, k in heads():
        vs = slice(h * GLA_DV, (h + 1) * GLA_DV)
        v = v_ref[:, vs]
        b = b_scr[:, ks]
        q_in = (q * jnp.exp(b)).astype(BF16)
        st = st_scr[h]
        o_inter = []
        for c in range(n_chunks):
            cr = slice(c * C, (c + 1) * C)
            o_inter.append(lax.dot_general(q_in[cr], st.astype(BF16), (((1,), (1,)), ((), ())),
                                           preferred_element_type=F32))
            b_last = b[(c + 1) * C - 1:(c + 1) * C]
            k_out = (k[cr] * jnp.exp(b_last - b[cr])).astype(BF16)
            upd = lax.dot_general(v[cr], k_out, (((0,), (0,)), ((), ())),
                                  preferred_element_type=F32)
            st = st * jnp.exp(b_last) + upd
        st_scr[h] = st
        o = (jnp.concatenate(o_inter, axis=0)
             + jnp.dot(a_scr[h].astype(BF16), v, preferred_element_type=F32))

        gate = gate_ref[:, vs].astype(F32)
        y = _rms(o, gn) * (gate * jax.nn.sigmoid(gate))
        o_ref[:, vs] = y.astype(BF16)


def _gla(proj, ga, wup, b_alpha, gla_norm):
    S = proj.shape[0]
    R = GLA_ROWS
    return pl.pallas_call(
        _gla_kernel,
        grid=(S // R,),
        in_specs=[
            pl.BlockSpec((R, GLA_QK), lambda i: (i, COL_GQ // GLA_QK)),
            pl.BlockSpec((R, GLA_QK), lambda i: (i, COL_GK // GLA_QK)),
            pl.BlockSpec((R, GLA_V), lambda i: (i, COL_GV // GLA_V)),
            pl.BlockSpec((R, GLA_V), lambda i: (i, COL_GG // GLA_V)),
            pl.BlockSpec((R, V7X_LANES), lambda i: (i, 0)),
            pl.BlockSpec((V7X_LANES, GLA_QK), lambda i: (0, 0)),
            pl.BlockSpec((1, GLA_QK), lambda i: (0, 0)),
            pl.BlockSpec((1, GLA_DV), lambda i: (0, 0)),
        ],
        out_specs=pl.BlockSpec((R, GLA_V), lambda i: (i, 0)),
        out_shape=jax.ShapeDtypeStruct((S, GLA_V), BF16),
        scratch_shapes=[
            pltpu.VMEM((GLA_HEADS, GLA_DV, GLA_DK), F32),
            pltpu.VMEM((R, GLA_QK), F32),
            pltpu.VMEM((GLA_HEADS, R, R), F32),
        ],
        compiler_params=_params(("arbitrary",)),
        name="gla",
    )(proj, proj, proj, proj, ga, wup, b_alpha, gla_norm)


def _attn_kernel(slopes_ref, lq1_ref, lk1_ref, lq2_ref, lk2_ref, gn_ref,
                 q_ref, k_ref, vt_ref, o_ref, feat_scr, qa_scr, s_scr, m_scr, l_scr, acc_scr):
    TQ, TK, QW, D = q_ref.shape[0], ATT_T, ATT_QW, DIFF_DQK
    h = pl.program_id(0)
    i = pl.program_id(1)
    n_kb = TQ // TK
    lane = lax.broadcasted_iota(jnp.int32, (TK, V7X_LANES), 1)

    @pl.when(i == 0)
    def _():
        slope2 = slopes_ref[h] * LOG2E
        row = lax.broadcasted_iota(jnp.int32, (TK, V7X_LANES), 0)

        def fill(blk, carry):
            r0 = pl.multiple_of(blk * TK, TK)
            b = slope2 * (row + r0).astype(F32)
            hi = b.astype(BF16).astype(F32)
            r1 = b - hi
            mid = r1.astype(BF16).astype(F32)
            lo = r1 - mid
            feat = jnp.where(lane == 0, hi,
                             jnp.where(lane == 1, mid, jnp.where(lane == 2, lo, 0.0)))
            feat_scr[pl.ds(r0, TK), :] = feat.astype(BF16)
            return carry

        lax.fori_loop(0, feat_scr.shape[0] // TK, fill, 0)

    m_scr[...] = jnp.full_like(m_scr, NEG_BIG)
    l_scr[...] = jnp.zeros_like(l_scr)
    acc_scr[...] = jnp.zeros_like(acc_scr)

    q_lane = lax.broadcasted_iota(jnp.int32, (TQ, V7X_LANES), 1)
    q_ones = jnp.where(q_lane < 3, 1.0, 0.0).astype(BF16)
    for c in range(2):
        qa_scr[c, :, 0:D] = q_ref[:, c * D:(c + 1) * D]
        qa_scr[c, :, D:2 * D] = q_ones
    key_row = lax.broadcasted_iota(jnp.int32, (TK, QW), 0)
    query_col = lax.broadcasted_iota(jnp.int32, (TK, QW), 1)

    def slabs(key_off):
        return [sl for sl in range(TQ // QW)
                if key_off is None or key_off <= sl * QW + QW - 1]

    def scores(j, slot, key_off):
        r0 = pl.multiple_of(j * TK, TK)
        feat = feat_scr[pl.ds(r0, TK), :]
        for c in range(2):
            k = jnp.concatenate([k_ref[pl.ds(r0, TK), c * D:(c + 1) * D], feat], axis=1)
            for sl in slabs(key_off):
                qs = slice(sl * QW, (sl + 1) * QW)
                s_scr[slot, c, :, qs] = lax.dot_general(
                    k, qa_scr[c, qs, :], (((1,), (1,)), ((), ())),
                    preferred_element_type=F32)

    def accumulate(j, slot, key_off):
        vt = vt_ref[j]
        for c in range(2):
            for sl in slabs(key_off):
                q_lo = sl * QW
                qs = slice(q_lo, q_lo + QW)
                s = s_scr[slot, c, :, qs]
                if key_off is not None and key_off + TK - 1 > q_lo:
                    s = jnp.where(key_row + key_off <= query_col + q_lo, s, NEG_BIG)
                m_prev = m_scr[c, :, qs]
                m_new = jnp.maximum(m_prev, jnp.max(s, axis=0, keepdims=True))
                alpha = jnp.exp2(m_prev - m_new)
                p = jnp.exp2(s - m_new)
                l_scr[c, :, qs] = alpha * l_scr[c, :, qs] + jnp.sum(p, axis=0, keepdims=True)
                acc_scr[c, :, qs] = alpha * acc_scr[c, :, qs] + jnp.dot(
                    vt, p.astype(BF16), preferred_element_type=F32)
                m_scr[c, :, qs] = m_new

    assert n_kb == 2
    scores(0, 0, None)

    def body(jj, carry):
        j0 = 2 * jj
        scores(j0 + 1, 1, None)
        accumulate(j0, 0, None)
        scores(j0 + 2, 0, None)
        accumulate(j0 + 1, 1, None)
        return carry

    lax.fori_loop(0, i, body, 0)
    scores(2 * i + 1, 1, TK)
    accumulate(2 * i, 0, 0)
    accumulate(2 * i + 1, 1, TK)

    lam = (jnp.exp(jnp.sum(lq1_ref[...] * lk1_ref[...], axis=-1, keepdims=True))
           - jnp.exp(jnp.sum(lq2_ref[...] * lk2_ref[...], axis=-1, keepdims=True))
           + LAMBDA_INIT)
    o1 = acc_scr[0] / l_scr[0]
    o2 = acc_scr[1] / l_scr[1]
    o = (o1 - lam * o2).T
    o_ref[...] = (_rms(o, gn_ref[...]) * (1.0 - LAMBDA_INIT)).astype(BF16)


def _diff_attn(proj, vt, slopes, lq1, lk1, lq2, lk2, diff_norm):
    S = proj.shape[0]
    TQ = min(ATT_TQ, S)
    W = 2 * DIFF_DQK
    vec = pl.BlockSpec((1, DIFF_DQK), lambda h, i: (0, 0))
    return pl.pallas_call(
        _attn_kernel,
        grid=(DIFF_HEADS, S // TQ),
        in_specs=[
            pl.BlockSpec(memory_space=pltpu.SMEM),
            vec, vec, vec, vec,
            pl.BlockSpec((1, DIFF_DV), lambda h, i: (0, 0)),
            pl.BlockSpec((TQ, W), lambda h, i: (i, COL_DQ // W + h)),
            pl.BlockSpec((S, W), lambda h, i: (0, COL_DK // W + h)),
            pl.BlockSpec((S // ATT_T, DIFF_DV, ATT_T), lambda h, i: (0, h, 0)),
        ],
        out_specs=pl.BlockSpec((TQ, DIFF_DV), lambda h, i: (i, h)),
        out_shape=jax.ShapeDtypeStruct((S, DIFF_V), BF16),
        scratch_shapes=[
            pltpu.VMEM((S, V7X_LANES), BF16),
            pltpu.VMEM((2, TQ, W), BF16),
            pltpu.VMEM((2, 2, ATT_T, TQ), F32),
            pltpu.VMEM((2, 1, TQ), F32),
            pltpu.VMEM((2, 1, TQ), F32),
            pltpu.VMEM((2, DIFF_DV, TQ), F32),
        ],
        compiler_params=_params(("arbitrary", "arbitrary")),
        name="diff_attn",
    )(slopes, lq1, lk1, lq2, lk2, diff_norm, proj, proj, vt)


def _out_proj_kernel(oa_ref, ob_ref, wo_ref, x_ref, gpost_ref, gpre_ref,
                     x1_ref, h2_ref):
    m = (jnp.dot(oa_ref[...], wo_ref[0:GLA_V, :], preferred_element_type=F32)
         + jnp.dot(ob_ref[...], wo_ref[GLA_V:GLA_V + DIFF_V, :],
                   preferred_element_type=F32))
    x1 = x_ref[...] + _rms(m, gpost_ref[...])
    x1_ref[...] = x1
    h2_ref[...] = _rms(x1, gpre_ref[...]).astype(BF16)


def _out_proj(o_a, o_b, w_o, x, g_post, g_pre):
    S = x.shape[0]
    tm = min(512, S)
    row = lambda i: (i, 0)
    fixed = lambda i: (0, 0)
    return pl.pallas_call(
        _out_proj_kernel,
        grid=(S // tm,),
        in_specs=[
            pl.BlockSpec((tm, GLA_V), row),
            pl.BlockSpec((tm, DIFF_V), row),
            pl.BlockSpec((GLA_V + DIFF_V, D_MODEL), fixed),
            pl.BlockSpec((tm, D_MODEL), row),
            pl.BlockSpec((1, D_MODEL), fixed),
            pl.BlockSpec((1, D_MODEL), fixed),
        ],
        out_specs=[
            pl.BlockSpec((tm, D_MODEL), row),
            pl.BlockSpec((tm, D_MODEL), row),
        ],
        out_shape=[
            jax.ShapeDtypeStruct((S, D_MODEL), F32),
            jax.ShapeDtypeStruct((S, D_MODEL), BF16),
        ],
        compiler_params=_params(("parallel",)),
        name="out_proj",
    )(o_a, o_b, w_o, x, g_post, g_pre)


def _ffn_kernel(h2_ref, wa_ref, wb_ref, cw_ref, cb_ref, wout_ref, x1_ref, gpost_ref,
                o_ref, acc_scr, abuf_scr, halo_scr, *, tm):
    i = pl.program_id(0)
    j = pl.program_id(1)
    nj = pl.num_programs(1)
    P = V7X_SUBLANES

    @pl.when((i == 0) & (j == 0))
    def _():
        acc_scr[...] = jnp.zeros_like(acc_scr)
        halo_scr[...] = jnp.zeros_like(halo_scr)

    abuf_scr[0:P, :] = halo_scr[j]
    for r0 in range(0, tm, FFN_RH):
        rs = slice(r0, r0 + FFN_RH)
        h2 = h2_ref[rs, :]
        contrib = None
        for c0 in range(0, wa_ref.shape[1], FFN_CW):
            cs = slice(c0, c0 + FFN_CW)
            a = jnp.dot(h2, wa_ref[:, cs], preferred_element_type=F32)
            b = jnp.dot(h2, wb_ref[:, cs], preferred_element_type=F32)
            abuf_scr[P + r0:P + r0 + FFN_RH, cs] = a
            a1 = abuf_scr[P - 1 + r0:P - 1 + r0 + FFN_RH, cs]
            a2 = abuf_scr[P - 2 + r0:P - 2 + r0 + FFN_RH, cs]
            conv = (cw_ref[0:1, cs] * a2 + cw_ref[1:2, cs] * a1 + cw_ref[2:3, cs] * a
                    + cb_ref[:, cs])
            g = (jax.nn.gelu(conv, approximate=True) * b).astype(BF16)
            part = jnp.dot(g, wout_ref[cs, :], preferred_element_type=F32)
            contrib = part if contrib is None else contrib + part
        acc_scr[rs, :] += contrib
    halo_scr[j] = abuf_scr[tm:tm + P, :]

    @pl.when(j == nj - 1)
    def _():
        o_ref[...] = x1_ref[...] + _rms(acc_scr[...], gpost_ref[...])
        acc_scr[...] = jnp.zeros_like(acc_scr)


def _ffn(h2, w_ffn_in, conv_w, conv_b, w_ffn_out, x1, g_post):
    S = h2.shape[0]
    tm = min(512, S)
    tn = 512
    nj = D_FF // tn
    row = lambda i, j: (i, 0)
    return pl.pallas_call(
        functools.partial(_ffn_kernel, tm=tm),
        grid=(S // tm, nj),
        in_specs=[
            pl.BlockSpec((tm, D_MODEL), row),
            pl.BlockSpec((D_MODEL, tn), lambda i, j: (0, j)),
            pl.BlockSpec((D_MODEL, tn), lambda i, j: (0, nj + j)),
            pl.BlockSpec((CONV_W, tn), lambda i, j: (0, j)),
            pl.BlockSpec((1, tn), lambda i, j: (0, j)),
            pl.BlockSpec((tn, D_MODEL), lambda i, j: (j, 0)),
            pl.BlockSpec((tm, D_MODEL), row),
            pl.BlockSpec((1, D_MODEL), lambda i, j: (0, 0)),
        ],
        out_specs=pl.BlockSpec((tm, D_MODEL), row),
        out_shape=jax.ShapeDtypeStruct((S, D_MODEL), F32),
        scratch_shapes=[
            pltpu.VMEM((tm, D_MODEL), F32),
            pltpu.VMEM((tm + V7X_SUBLANES, tn), F32),
            pltpu.VMEM((nj, V7X_SUBLANES, tn), F32),
        ],
        compiler_params=_params(("arbitrary", "arbitrary")),
        name="ffn",
    )(h2, w_ffn_in, w_ffn_in, conv_w, conv_b, w_ffn_out, x1, g_post)


def _layer(x, attn_pre_norm, w_in, w_alpha_up, b_alpha, gla_norm, lambda_q1, lambda_k1,
           lambda_q2, lambda_k2, diff_norm, w_o, attn_post_norm, ffn_pre_norm, w_ffn_in,
           conv_w, conv_b, w_ffn_out, ffn_post_norm):
    vec = lambda p: p.reshape(1, -1).astype(F32)

    w_main = jnp.concatenate(
        [w_in[:, :GA_OFFSET], w_in[:, GA_OFFSET + GLA_RANK:]], axis=1).astype(BF16)
    w_ga = jnp.pad(w_in[:, GA_OFFSET:GA_OFFSET + GLA_RANK],
                   ((0, 0), (0, V7X_LANES - GLA_RANK))).astype(BF16)
    colscale = jnp.concatenate([
        jnp.full((GLA_QK,), GLA_DK ** -0.5, F32),
        jnp.ones((COL_DQ - COL_GK,), F32),
        jnp.full((DIFF_QK,), DIFF_DQK ** -0.5 * LOG2E, F32),
        jnp.ones((PROJ_COLS - COL_DK,), F32),
    ]).reshape(1, PROJ_COLS)
    wup = jnp.pad(w_alpha_up, ((0, V7X_LANES - GLA_RANK), (0, 0))).astype(BF16)
    slopes = jnp.asarray(
        [2.0 ** (-8.0 * (h + 1) / DIFF_HEADS) for h in range(DIFF_HEADS)], F32)

    proj, ga, vt = _in_proj(x, vec(attn_pre_norm), w_main, w_ga, colscale)
    o_a = _gla(proj, ga, wup, vec(b_alpha), vec(gla_norm))
    o_b = _diff_attn(proj, vt, slopes, vec(lambda_q1), vec(lambda_k1), vec(lambda_q2),
                     vec(lambda_k2), vec(diff_norm))
    x1, h2 = _out_proj(o_a, o_b, w_o.astype(BF16), x, vec(attn_post_norm),
                       vec(ffn_pre_norm))
    return _ffn(h2, w_ffn_in.astype(BF16), conv_w.astype(F32), vec(conv_b),
                w_ffn_out.astype(BF16), x1, vec(ffn_post_norm))


def kernel(x, attn_pre_norm, w_in, w_alpha_up, b_alpha, gla_norm, lambda_q1, lambda_k1,
           lambda_q2, lambda_k2, diff_norm, w_o, attn_post_norm, ffn_pre_norm, w_ffn_in,
           conv_w, conv_b, w_ffn_out, ffn_post_norm):
    B = x.shape[0]
    depth = w_in.shape[0]
    assert depth == 1, "lambda_init is baked for a single layer"
    outs = []
    for bi in range(B):
        xb = x[bi]
        for l in range(depth):
            xb = _layer(xb, attn_pre_norm[l], w_in[l], w_alpha_up[l], b_alpha[l], gla_norm[l],
                        lambda_q1[l], lambda_k1[l], lambda_q2[l], lambda_k2[l], diff_norm[l],
                        w_o[l], attn_post_norm[l], ffn_pre_norm[l], w_ffn_in[l], conv_w[l],
                        conv_b[l], w_ffn_out[l], ffn_post_norm[l])
        outs.append(xb)
    return outs[0][None] if B == 1 else jnp.stack(outs, axis=0)
```

```python
import functools
import math

import jax
import jax.numpy as jnp
from jax import lax
from jax.experimental import pallas as pl
from jax.experimental.pallas import tpu as pltpu

F32 = jnp.float32
BF16 = jnp.bfloat16

D_MODEL = 2048
GLA_HEADS = 4
GLA_DK = 128
GLA_DV = 256
GLA_RANK = 16
GLA_TAU = 16.0
DIFF_HEADS = 4
DIFF_DQK = 128
DIFF_DV = 256
D_FF = 5632
CONV_W = 3
EPS = 1e-6
LAMBDA_INIT = 0.8 - 0.6 * math.exp(-0.3 * 0)
LOG2E = math.log2(math.e)

GLA_QK = GLA_HEADS * GLA_DK
GLA_V = GLA_HEADS * GLA_DV
DIFF_QK = DIFF_HEADS * 2 * DIFF_DQK
DIFF_V = DIFF_HEADS * DIFF_DV

COL_GQ = 0
COL_GK = COL_GQ + GLA_QK
COL_GV = COL_GK + GLA_QK
COL_GG = COL_GV + GLA_V
COL_DQ = COL_GG + GLA_V
COL_DK = COL_DQ + DIFF_QK
COL_DV = COL_DK + DIFF_QK
PROJ_COLS = COL_DV + DIFF_V
GA_OFFSET = GLA_QK + GLA_QK + GLA_V + GLA_V

V7X_LANES = 128
V7X_SUBLANES = 8
V7X_VMEM_LIMIT_BYTES = 56 * 1024 * 1024

FFN_CW = 256
FFN_RH = 512
ATT_T = 512
ATT_TQ = 1024
ATT_QW = 256
ATT_SKIP_LOG2 = 160.0
ATT_NORM_SLACK = 1.001
GLA_CHUNK = 64
GLA_ROWS = 256
GLA_SUB = 16
GLA_SAFE_SPAN = 60.0
NEG_BIG = -1e30


def _rms(v, g):
    return v * lax.rsqrt(jnp.mean(v * v, axis=-1, keepdims=True) + EPS) * g


def _params(dims):
    return pltpu.CompilerParams(dimension_semantics=dims,
                                vmem_limit_bytes=V7X_VMEM_LIMIT_BYTES)


def _in_proj_kernel(x_ref, g_ref, w_ref, wga_ref, cs_ref, o_ref, ga_ref, vt_ref, h_scr,
                    *, tn):
    j = pl.program_id(1)

    @pl.when(j == 0)
    def _():
        h = _rms(x_ref[...], g_ref[...]).astype(BF16)
        h_scr[...] = h
        ga_ref[...] = jnp.dot(h, wga_ref[...], preferred_element_type=F32)

    acc = jnp.dot(h_scr[...], w_ref[...], preferred_element_type=F32)
    o_ref[...] = (acc * cs_ref[...]).astype(BF16)

    @pl.when(j == COL_DV // tn)
    def _():
        acc_t = acc.T.astype(BF16)
        for kb in range(vt_ref.shape[0]):
            vt_ref[kb] = acc_t[:, kb * ATT_T:(kb + 1) * ATT_T]


def _in_proj(x, g, w_main, w_ga, colscale):
    S = x.shape[0]
    tm = min(1024, S)
    tn = DIFF_V
    return pl.pallas_call(
        functools.partial(_in_proj_kernel, tn=tn),
        grid=(S // tm, PROJ_COLS // tn),
        in_specs=[
            pl.BlockSpec((tm, D_MODEL), lambda i, j: (i, 0)),
            pl.BlockSpec((1, D_MODEL), lambda i, j: (0, 0)),
            pl.BlockSpec((D_MODEL, tn), lambda i, j: (0, j)),
            pl.BlockSpec((D_MODEL, V7X_LANES), lambda i, j: (0, 0)),
            pl.BlockSpec((1, tn), lambda i, j: (0, j)),
        ],
        out_specs=[
            pl.BlockSpec((tm, tn), lambda i, j: (i, j)),
            pl.BlockSpec((tm, V7X_LANES), lambda i, j: (i, 0)),
            pl.BlockSpec((tm // ATT_T, DIFF_V, ATT_T), lambda i, j: (i, 0, 0)),
        ],
        out_shape=[
            jax.ShapeDtypeStruct((S, PROJ_COLS), BF16),
            jax.ShapeDtypeStruct((S, V7X_LANES), F32),
            jax.ShapeDtypeStruct((S // ATT_T, DIFF_V, ATT_T), BF16),
        ],
        scratch_shapes=[pltpu.VMEM((tm, D_MODEL), BF16)],
        compiler_params=_params(("parallel", "arbitrary")),
        name="in_proj",
    )(x, g, w_main, w_ga, colscale)


def _log_sigmoid(x):
    return jnp.minimum(x, 0.0) - jnp.log(1.0 + jnp.exp(-jnp.abs(x)))


def _gla_kernel(q_ref, k_ref, v_ref, gate_ref, ga_ref, wup_ref, ba_ref, gn_ref,
                o_ref, st_scr, b_scr, a_scr):
    C, R = GLA_CHUNK, GLA_ROWS
    n_chunks = R // C

    @pl.when(pl.program_id(0) == 0)
    def _():
        st_scr[...] = jnp.zeros_like(st_scr)

    row = lax.broadcasted_iota(jnp.int32, (R, R), 0)
    col = lax.broadcasted_iota(jnp.int32, (R, R), 1)
    shift = C.bit_length() - 1
    in_chunk = (col <= row) & ((row >> shift) == (col >> shift))
    tril = jnp.where(in_chunk, 1.0, 0.0).astype(BF16)
    gn = gn_ref[...]

    z = jnp.dot(ga_ref[...].astype(BF16), wup_ref[...], preferred_element_type=F32) + ba_ref[...]
    la = _log_sigmoid(z) * (1.0 / GLA_TAU)
    la_hi = la.astype(BF16)
    la_lo = (la - la_hi.astype(F32)).astype(BF16)
    b_all = (jnp.dot(tril, la_hi, preferred_element_type=F32)
             + jnp.dot(tril, la_lo, preferred_element_type=F32))
    b_scr[...] = b_all
    b_first = jnp.concatenate(
        [jnp.broadcast_to(b_all[c * C:c * C + 1], (C, GLA_QK)) for c in range(n_chunks)], axis=0)
    span = jnp.max(b_first - b_all)
    small_span = span <= GLA_SAFE_SPAN

    def heads():
        for h in range(GLA_HEADS):
            ks = slice(h * GLA_DK, (h + 1) * GLA_DK)
            yield h, ks, q_ref[:, ks].astype(F32), k_ref[:, ks].astype(F32)

    @pl.when(small_span)
    def _():
        for h, ks, q, k in heads():
            d = b_first[:, ks] - b_scr[:, ks]
            q_t = (q * jnp.exp(-d)).astype(BF16)
            k_t = (k * jnp.exp(d)).astype(BF16)
            s = lax.dot_general(q_t, k_t, (((1,), (1,)), ((), ())),
                                preferred_element_type=F32)
            a_scr[h] = jnp.where(in_chunk, s, 0.0)

    @pl.when(jnp.logical_not(small_span))
    def _():
        a_scr[...] = jnp.zeros_like(a_scr)
        sub_row = lax.broadcasted_iota(jnp.int32, (GLA_SUB, C), 0)
        sub_col = lax.broadcasted_iota(jnp.int32, (GLA_SUB, C), 1)
        for h, ks, q_all, k_all in heads():
            for c in range(n_chunks):
                q = q_all[c * C:(c + 1) * C]
                k = k_all[c * C:(c + 1) * C]
                b = b_scr[c * C:(c + 1) * C, ks]
                for blk in range(C // GLA_SUB):
                    r0 = blk * GLA_SUB
                    b_blk = b[r0:r0 + GLA_SUB]
                    q_blk = q[r0:r0 + GLA_SUB]
                    b_ref0 = b[r0:r0 + 1]
                    a_blk = jnp.zeros((GLA_SUB, C), F32)
                    if blk > 0:
                        q_t = (q_blk * jnp.exp(b_blk - b_ref0)).astype(BF16)
                        k_t = (k * jnp.exp(jnp.minimum(b_ref0 - b, 0.0))).astype(BF16)
                        off = lax.dot_general(q_t, k_t, (((1,), (1,)), ((), ())),
                                              preferred_element_type=F32)
                        a_blk = jnp.where(sub_col < r0, off, 0.0)
                    for jj in range(GLA_SUB):
                        kj = k[r0 + jj:r0 + jj + 1]
                        bj = b[r0 + jj:r0 + jj + 1]
                        t = q_blk * kj * jnp.exp(jnp.minimum(b_blk - bj, 0.0))
                        cj = jnp.sum(t, axis=-1, keepdims=True)
                        a_blk = jnp.where((sub_col == r0 + jj) & (sub_row >= jj), cj, a_blk)
                    a_scr[h, c * C + r0:c * C + r0 + GLA_SUB, c * C:(c + 1) * C] = a_blk

    for h, ks, q, k in heads():
        vs = slice(h * GLA_DV, (h + 1) * GLA_DV)
        v = v_ref[:, vs]
        b = b_scr[:, ks]
        q_in = (q * jnp.exp(b)).astype(BF16)
        st = st_scr[h]
        o_inter = []
        for c in range(n_chunks):
            cr = slice(c * C, (c + 1) * C)
            o_inter.append(lax.dot_general(q_in[cr], st.astype(BF16), (((1,), (1,)), ((), ())),
                                           preferred_element_type=F32))
            b_last = b[(c + 1) * C - 1:(c + 1) * C]
            k_out = (k[cr] * jnp.exp(b_last - b[cr])).astype(BF16)
            upd = lax.dot_general(v[cr], k_out, (((0,), (0,)), ((), ())),
                                  preferred_element_type=F32)
            st = st * jnp.exp(b_last) + upd
        st_scr[h] = st
        o = (jnp.concatenate(o_inter, axis=0)
             + jnp.dot(a_scr[h].astype(BF16), v, preferred_element_type=F32))

        gate = gate_ref[:, vs].astype(F32)
        y = _rms(o, gn) * (gate * jax.nn.sigmoid(gate))
        o_ref[:, vs] = y.astype(BF16)


def _gla(proj, ga, wup, b_alpha, gla_norm):
    S = proj.shape[0]
    R = GLA_ROWS
    return pl.pallas_call(
        _gla_kernel,
        grid=(S // R,),
        in_specs=[
            pl.BlockSpec((R, GLA_QK), lambda i: (i, COL_GQ // GLA_QK)),
            pl.BlockSpec((R, GLA_QK), lambda i: (i, COL_GK // GLA_QK)),
            pl.BlockSpec((R, GLA_V), lambda i: (i, COL_GV // GLA_V)),
            pl.BlockSpec((R, GLA_V), lambda i: (i, COL_GG // GLA_V)),
            pl.BlockSpec((R, V7X_LANES), lambda i: (i, 0)),
            pl.BlockSpec((V7X_LANES, GLA_QK), lambda i: (0, 0)),
            pl.BlockSpec((1, GLA_QK), lambda i: (0, 0)),
            pl.BlockSpec((1, GLA_DV), lambda i: (0, 0)),
        ],
        out_specs=pl.BlockSpec((R, GLA_V), lambda i: (i, 0)),
        out_shape=jax.ShapeDtypeStruct((S, GLA_V), BF16),
        scratch_shapes=[
            pltpu.VMEM((GLA_HEADS, GLA_DV, GLA_DK), F32),
            pltpu.VMEM((R, GLA_QK), F32),
            pltpu.VMEM((GLA_HEADS, R, R), F32),
        ],
        compiler_params=_params(("arbitrary",)),
        name="gla",
    )(proj, proj, proj, proj, ga, wup, b_alpha, gla_norm)


def _attn_kernel(slopes_ref, lq1_ref, lk1_ref, lq2_ref, lk2_ref, gn_ref,
                 q_ref, k_ref, vt_ref, o_ref, feat_scr, kn_scr, qa_scr, s_scr, m_scr, l_scr,
                 acc_scr):
    TQ, TK, QW, D = q_ref.shape[0], ATT_T, ATT_QW, DIFF_DQK
    h = pl.program_id(0)
    i = pl.program_id(1)
    n_kb = TQ // TK
    lane = lax.broadcasted_iota(jnp.int32, (TK, V7X_LANES), 1)

    slope2 = slopes_ref[h] * LOG2E
    blk_lane = lax.broadcasted_iota(jnp.int32, (1, V7X_LANES), 1)

    @pl.when(i == 0)
    def _():
        row = lax.broadcasted_iota(jnp.int32, (TK, V7X_LANES), 0)

        def fill(blk, carry):
            r0 = pl.multiple_of(blk * TK, TK)
            b = slope2 * (row + r0).astype(F32)
            hi = b.astype(BF16).astype(F32)
            r1 = b - hi
            mid = r1.astype(BF16).astype(F32)
            lo = r1 - mid
            feat = jnp.where(lane == 0, hi,
                             jnp.where(lane == 1, mid, jnp.where(lane == 2, lo, 0.0)))
            feat_scr[pl.ds(r0, TK), :] = feat.astype(BF16)
            new = []
            for c in range(2):
                run, vec = carry[2 * c], carry[2 * c + 1]
                kc = k_ref[pl.ds(r0, TK), c * D:(c + 1) * D].astype(F32)
                sq = jnp.sum(kc * kc, axis=-1, keepdims=True)
                run = jnp.maximum(run, jnp.max(sq, axis=0, keepdims=True))
                new += [run, jnp.where(blk_lane == blk, jnp.sqrt(run), vec)]
            return tuple(new)

        zero = jnp.zeros((1, 1), F32)
        zvec = jnp.zeros((1, V7X_LANES), F32)
        done = lax.fori_loop(0, feat_scr.shape[0] // TK, fill, (zero, zvec, zero, zvec))
        kn_scr[0] = done[1]
        kn_scr[1] = done[3]

    m_scr[...] = jnp.full_like(m_scr, NEG_BIG)
    l_scr[...] = jnp.zeros_like(l_scr)
    acc_scr[...] = jnp.zeros_like(acc_scr)

    q_lane = lax.broadcasted_iota(jnp.int32, (TQ, V7X_LANES), 1)
    q_ones = jnp.where(q_lane < 3, 1.0, 0.0).astype(BF16)
    for c in range(2):
        qa_scr[c, :, 0:D] = q_ref[:, c * D:(c + 1) * D]
        qa_scr[c, :, D:2 * D] = q_ones
    key_row = lax.broadcasted_iota(jnp.int32, (TK, QW), 0)
    query_col = lax.broadcasted_iota(jnp.int32, (TK, QW), 1)

    def slabs(key_off):
        return [sl for sl in range(TQ // QW)
                if key_off is None or key_off <= sl * QW + QW - 1]

    def scores(j, slot, key_off):
        r0 = pl.multiple_of(j * TK, TK)
        feat = feat_scr[pl.ds(r0, TK), :]
        for c in range(2):
            k = jnp.concatenate([k_ref[pl.ds(r0, TK), c * D:(c + 1) * D], feat], axis=1)
            for sl in slabs(key_off):
                qs = slice(sl * QW, (sl + 1) * QW)
                s_scr[slot, c, :, qs] = lax.dot_general(
                    k, qa_scr[c, qs, :], (((1,), (1,)), ((), ())),
                    preferred_element_type=F32)

    def accumulate(j, slot, key_off):
        vt = vt_ref[j]
        for c in range(2):
            for sl in slabs(key_off):
                q_lo = sl * QW
                qs = slice(q_lo, q_lo + QW)
                s = s_scr[slot, c, :, qs]
                if key_off is not None and key_off + TK - 1 > q_lo:
                    s = jnp.where(key_row + key_off <= query_col + q_lo, s, NEG_BIG)
                m_prev = m_scr[c, :, qs]
                m_new = jnp.maximum(m_prev, jnp.max(s, axis=0, keepdims=True))
                alpha = jnp.exp2(m_prev - m_new)
                p = jnp.exp2(s - m_new)
                l_scr[c, :, qs] = alpha * l_scr[c, :, qs] + jnp.sum(p, axis=0, keepdims=True)
                acc_scr[c, :, qs] = alpha * acc_scr[c, :, qs] + jnp.dot(
                    vt, p.astype(BF16), preferred_element_type=F32)
                m_scr[c, :, qs] = m_new

    assert n_kb == 2
    scores(2 * i, 0, 0)
    scores(2 * i + 1, 1, TK)
    accumulate(2 * i + 1, 1, TK)
    scores(jnp.maximum(2 * i - 1, 0), 1, None)
    accumulate(2 * i, 0, 0)

    last_pos = (blk_lane * TK + (TK - 1)).astype(F32)
    skip = blk_lane < 2 * i
    for c in range(2):
        qf = q_ref[:, c * D:(c + 1) * D].astype(F32)
        qn = jnp.sqrt(jnp.max(jnp.sum(qf * qf, axis=-1, keepdims=True), axis=0, keepdims=True))
        bound = qn * kn_scr[c] * ATT_NORM_SLACK + slope2 * last_pos
        m_min = jnp.min(m_scr[c], axis=-1, keepdims=True)
        skip = skip & (bound < m_min - ATT_SKIP_LOG2)
    first_kept = jnp.min(jnp.where(skip, float(V7X_LANES), blk_lane.astype(F32)))
    n_skip = jnp.minimum(first_kept.astype(jnp.int32), 2 * i)
    n_pairs = i - n_skip // 2

    def body(t, carry):
        lo = 2 * (i - 1 - t)
        scores(lo, 0, None)
        accumulate(lo + 1, 1, None)
        scores(jnp.maximum(lo - 1, 0), 1, None)
        accumulate(lo, 0, None)
        return carry

    lax.fori_loop(0, n_pairs, body, 0)

    lam = (jnp.exp(jnp.sum(lq1_ref[...] * lk1_ref[...], axis=-1, keepdims=True))
           - jnp.exp(jnp.sum(lq2_ref[...] * lk2_ref[...], axis=-1, keepdims=True))
           + LAMBDA_INIT)
    o1 = acc_scr[0] / l_scr[0]
    o2 = acc_scr[1] / l_scr[1]
    o = (o1 - lam * o2).T
    o_ref[...] = (_rms(o, gn_ref[...]) * (1.0 - LAMBDA_INIT)).astype(BF16)


def _diff_attn(proj, vt, slopes, lq1, lk1, lq2, lk2, diff_norm):
    S = proj.shape[0]
    TQ = min(ATT_TQ, S)
    W = 2 * DIFF_DQK
    vec = pl.BlockSpec((1, DIFF_DQK), lambda h, i: (0, 0))
    return pl.pallas_call(
        _attn_kernel,
        grid=(DIFF_HEADS, S // TQ),
        in_specs=[
            pl.BlockSpec(memory_space=pltpu.SMEM),
            vec, vec, vec, vec,
            pl.BlockSpec((1, DIFF_DV), lambda h, i: (0, 0)),
            pl.BlockSpec((TQ, W), lambda h, i: (i, COL_DQ // W + h)),
            pl.BlockSpec((S, W), lambda h, i: (0, COL_DK // W + h)),
            pl.BlockSpec((S // ATT_T, DIFF_DV, ATT_T), lambda h, i: (0, h, 0)),
        ],
        out_specs=pl.BlockSpec((TQ, DIFF_DV), lambda h, i: (i, h)),
        out_shape=jax.ShapeDtypeStruct((S, DIFF_V), BF16),
        scratch_shapes=[
            pltpu.VMEM((S, V7X_LANES), BF16),
            pltpu.VMEM((2, 1, V7X_LANES), F32),
            pltpu.VMEM((2, TQ, W), BF16),
            pltpu.VMEM((2, 2, ATT_T, TQ), F32),
            pltpu.VMEM((2, 1, TQ), F32),
            pltpu.VMEM((2, 1, TQ), F32),
            pltpu.VMEM((2, DIFF_DV, TQ), F32),
        ],
        compiler_params=_params(("arbitrary", "arbitrary")),
        name="diff_attn",
    )(slopes, lq1, lk1, lq2, lk2, diff_norm, proj, proj, vt)


def _out_proj_kernel(oa_ref, ob_ref, wo_ref, x_ref, gpost_ref, gpre_ref,
                     x1_ref, h2_ref):
    m = (jnp.dot(oa_ref[...], wo_ref[0:GLA_V, :], preferred_element_type=F32)
         + jnp.dot(ob_ref[...], wo_ref[GLA_V:GLA_V + DIFF_V, :],
                   preferred_element_type=F32))
    x1 = x_ref[...] + _rms(m, gpost_ref[...])
    x1_ref[...] = x1
    h2_ref[...] = _rms(x1, gpre_ref[...]).astype(BF16)


def _out_proj(o_a, o_b, w_o, x, g_post, g_pre):
    S = x.shape[0]
    tm = min(512, S)
    row = lambda i: (i, 0)
    fixed = lambda i: (0, 0)
    return pl.pallas_call(
        _out_proj_kernel,
        grid=(S // tm,),
        in_specs=[
            pl.BlockSpec((tm, GLA_V), row),
            pl.BlockSpec((tm, DIFF_V), row),
            pl.BlockSpec((GLA_V + DIFF_V, D_MODEL), fixed),
            pl.BlockSpec((tm, D_MODEL), row),
            pl.BlockSpec((1, D_MODEL), fixed),
            pl.BlockSpec((1, D_MODEL), fixed),
        ],
        out_specs=[
            pl.BlockSpec((tm, D_MODEL), row),
            pl.BlockSpec((tm, D_MODEL), row),
        ],
        out_shape=[
            jax.ShapeDtypeStruct((S, D_MODEL), F32),
            jax.ShapeDtypeStruct((S, D_MODEL), BF16),
        ],
        compiler_params=_params(("parallel",)),
        name="out_proj",
    )(o_a, o_b, w_o, x, g_post, g_pre)


def _ffn_kernel(h2_ref, wa_ref, wb_ref, cw_ref, cb_ref, wout_ref, x1_ref, gpost_ref,
                o_ref, acc_scr, abuf_scr, halo_scr, *, tm):
    i = pl.program_id(0)
    j = pl.program_id(1)
    nj = pl.num_programs(1)
    P = V7X_SUBLANES

    @pl.when((i == 0) & (j == 0))
    def _():
        acc_scr[...] = jnp.zeros_like(acc_scr)
        halo_scr[...] = jnp.zeros_like(halo_scr)

    abuf_scr[0:P, :] = halo_scr[j]
    for r0 in range(0, tm, FFN_RH):
        rs = slice(r0, r0 + FFN_RH)
        h2 = h2_ref[rs, :]
        contrib = None
        for c0 in range(0, wa_ref.shape[1], FFN_CW):
            cs = slice(c0, c0 + FFN_CW)
            a = jnp.dot(h2, wa_ref[:, cs], preferred_element_type=F32)
            b = jnp.dot(h2, wb_ref[:, cs], preferred_element_type=F32)
            abuf_scr[P + r0:P + r0 + FFN_RH, cs] = a
            a1 = abuf_scr[P - 1 + r0:P - 1 + r0 + FFN_RH, cs]
            a2 = abuf_scr[P - 2 + r0:P - 2 + r0 + FFN_RH, cs]
            conv = (cw_ref[0:1, cs] * a2 + cw_ref[1:2, cs] * a1 + cw_ref[2:3, cs] * a
                    + cb_ref[:, cs])
            g = (jax.nn.gelu(conv, approximate=True) * b).astype(BF16)
            part = jnp.dot(g, wout_ref[cs, :], preferred_element_type=F32)
            contrib = part if contrib is None else contrib + part
        acc_scr[rs, :] += contrib
    halo_scr[j] = abuf_scr[tm:tm + P, :]

    @pl.when(j == nj - 1)
    def _():
        o_ref[...] = x1_ref[...] + _rms(acc_scr[...], gpost_ref[...])
        acc_scr[...] = jnp.zeros_like(acc_scr)


def _ffn(h2, w_ffn_in, conv_w, conv_b, w_ffn_out, x1, g_post):
    S = h2.shape[0]
    tm = min(512, S)
    tn = 512
    nj = D_FF // tn
    row = lambda i, j: (i, 0)
    return pl.pallas_call(
        functools.partial(_ffn_kernel, tm=tm),
        grid=(S // tm, nj),
        in_specs=[
            pl.BlockSpec((tm, D_MODEL), row),
            pl.BlockSpec((D_MODEL, tn), lambda i, j: (0, j)),
            pl.BlockSpec((D_MODEL, tn), lambda i, j: (0, nj + j)),
            pl.BlockSpec((CONV_W, tn), lambda i, j: (0, j)),
            pl.BlockSpec((1, tn), lambda i, j: (0, j)),
            pl.BlockSpec((tn, D_MODEL), lambda i, j: (j, 0)),
            pl.BlockSpec((tm, D_MODEL), row),
            pl.BlockSpec((1, D_MODEL), lambda i, j: (0, 0)),
        ],
        out_specs=pl.BlockSpec((tm, D_MODEL), row),
        out_shape=jax.ShapeDtypeStruct((S, D_MODEL), F32),
        scratch_shapes=[
            pltpu.VMEM((tm, D_MODEL), F32),
            pltpu.VMEM((tm + V7X_SUBLANES, tn), F32),
            pltpu.VMEM((nj, V7X_SUBLANES, tn), F32),
        ],
        compiler_params=_params(("arbitrary", "arbitrary")),
        name="ffn",
    )(h2, w_ffn_in, w_ffn_in, conv_w, conv_b, w_ffn_out, x1, g_post)


def _layer(x, attn_pre_norm, w_in, w_alpha_up, b_alpha, gla_norm, lambda_q1, lambda_k1,
           lambda_q2, lambda_k2, diff_norm, w_o, attn_post_norm, ffn_pre_norm, w_ffn_in,
           conv_w, conv_b, w_ffn_out, ffn_post_norm):
    vec = lambda p: p.reshape(1, -1).astype(F32)

    w_main = jnp.concatenate(
        [w_in[:, :GA_OFFSET], w_in[:, GA_OFFSET + GLA_RANK:]], axis=1).astype(BF16)
    w_ga = jnp.pad(w_in[:, GA_OFFSET:GA_OFFSET + GLA_RANK],
                   ((0, 0), (0, V7X_LANES - GLA_RANK))).astype(BF16)
    colscale = jnp.concatenate([
        jnp.full((GLA_QK,), GLA_DK ** -0.5, F32),
        jnp.ones((COL_DQ - COL_GK,), F32),
        jnp.full((DIFF_QK,), DIFF_DQK ** -0.5 * LOG2E, F32),
        jnp.ones((PROJ_COLS - COL_DK,), F32),
    ]).reshape(1, PROJ_COLS)
    wup = jnp.pad(w_alpha_up, ((0, V7X_LANES - GLA_RANK), (0, 0))).astype(BF16)
    slopes = jnp.asarray(
        [2.0 ** (-8.0 * (h + 1) / DIFF_HEADS) for h in range(DIFF_HEADS)], F32)

    proj, ga, vt = _in_proj(x, vec(attn_pre_norm), w_main, w_ga, colscale)
    o_a = _gla(proj, ga, wup, vec(b_alpha), vec(gla_norm))
    o_b = _diff_attn(proj, vt, slopes, vec(lambda_q1), vec(lambda_k1), vec(lambda_q2),
                     vec(lambda_k2), vec(diff_norm))
    x1, h2 = _out_proj(o_a, o_b, w_o.astype(BF16), x, vec(attn_post_norm),
                       vec(ffn_pre_norm))
    return _ffn(h2, w_ffn_in.astype(BF16), conv_w.astype(F32), vec(conv_b),
                w_ffn_out.astype(BF16), x1, vec(ffn_post_norm))


def kernel(x, attn_pre_norm, w_in, w_alpha_up, b_alpha, gla_norm, lambda_q1, lambda_k1,
           lambda_q2, lambda_k2, diff_norm, w_o, attn_post_norm, ffn_pre_norm, w_ffn_in,
           conv_w, conv_b, w_ffn_out, ffn_post_norm):
    B = x.shape[0]
    depth = w_in.shape[0]
    assert depth == 1, "lambda_init is baked for a single layer"
    outs = []
    for bi in range(B):
        xb = x[bi]
        for l in range(depth):
            xb = _layer(xb, attn_pre_norm[l], w_in[l], w_alpha_up[l], b_alpha[l], gla_norm[l],
                        lambda_q1[l], lambda_k1[l], lambda_q2[l], lambda_k2[l], diff_norm[l],
                        w_o[l], attn_post_norm[l], ffn_pre_norm[l], w_ffn_in[l], conv_w[l],
                        conv_b[l], w_ffn_out[l], ffn_post_norm[l])
        outs.append(xb)
    return outs[0][None] if B == 1 else jnp.stack(outs, axis=0)
```

```python
import functools
import math

import jax
import jax.numpy as jnp
from jax import lax
from jax.experimental import pallas as pl
from jax.experimental.pallas import tpu as pltpu

F32 = jnp.float32
BF16 = jnp.bfloat16

D_MODEL = 2048
GLA_HEADS = 4
GLA_DK = 128
GLA_DV = 256
GLA_RANK = 16
GLA_TAU = 16.0
DIFF_HEADS = 4
DIFF_DQK = 128
DIFF_DV = 256
D_FF = 5632
CONV_W = 3
EPS = 1e-6
LAMBDA_INIT = 0.8 - 0.6 * math.exp(-0.3 * 0)
LOG2E = math.log2(math.e)

GLA_QK = GLA_HEADS * GLA_DK
GLA_V = GLA_HEADS * GLA_DV
DIFF_QK = DIFF_HEADS * 2 * DIFF_DQK
DIFF_V = DIFF_HEADS * DIFF_DV

COL_GQ = 0
COL_GK = COL_GQ + GLA_QK
COL_GV = COL_GK + GLA_QK
COL_GG = COL_GV + GLA_V
COL_DQ = COL_GG + GLA_V
COL_DK = COL_DQ + DIFF_QK
COL_DV = COL_DK + DIFF_QK
PROJ_COLS = COL_DV + DIFF_V
GA_OFFSET = GLA_QK + GLA_QK + GLA_V + GLA_V

V7X_LANES = 128
V7X_SUBLANES = 8
V7X_VMEM_LIMIT_BYTES = 56 * 1024 * 1024

IN_TN = 1536
OUT_RH = 256
FFN_CW = 256
FFN_RH = 512
ATT_T = 512
ATT_TQ = 1024
ATT_QW = 256
ATT_SKIP_LOG2 = 160.0
ATT_NORM_SLACK = 1.001
GLA_CHUNK = 64
GLA_ROWS = 256
GLA_SUB = 16
GLA_SAFE_SPAN = 60.0
NEG_BIG = -1e30


def _rms(v, g):
    return v * lax.rsqrt(jnp.mean(v * v, axis=-1, keepdims=True) + EPS) * g


def _params(dims):
    return pltpu.CompilerParams(dimension_semantics=dims,
                                vmem_limit_bytes=V7X_VMEM_LIMIT_BYTES)


def _in_proj_kernel(x_ref, g_ref, w_ref, wga_ref, cs_ref, o_ref, ga_ref, vt_ref, h_scr,
                    *, tn):
    j = pl.program_id(1)

    @pl.when(j == 0)
    def _():
        h = _rms(x_ref[...], g_ref[...]).astype(BF16)
        h_scr[...] = h
        ga_ref[...] = jnp.dot(h, wga_ref[...], preferred_element_type=F32)

    acc = jnp.dot(h_scr[...], w_ref[...], preferred_element_type=F32)
    o_ref[...] = (acc * cs_ref[...]).astype(BF16)

    @pl.when(j == COL_DV // tn)
    def _():
        v0 = COL_DV % tn
        acc_t = acc[:, v0:v0 + DIFF_V].T.astype(BF16)
        for kb in range(vt_ref.shape[0]):
            vt_ref[kb] = acc_t[:, kb * ATT_T:(kb + 1) * ATT_T]


def _in_proj(x, g, w_main, w_ga, colscale):
    S = x.shape[0]
    tm = min(1024, S)
    tn = IN_TN
    return pl.pallas_call(
        functools.partial(_in_proj_kernel, tn=tn),
        grid=(S // tm, PROJ_COLS // tn),
        in_specs=[
            pl.BlockSpec((tm, D_MODEL), lambda i, j: (i, 0)),
            pl.BlockSpec((1, D_MODEL), lambda i, j: (0, 0)),
            pl.BlockSpec((D_MODEL, tn), lambda i, j: (0, j)),
            pl.BlockSpec((D_MODEL, V7X_LANES), lambda i, j: (0, 0)),
            pl.BlockSpec((1, tn), lambda i, j: (0, j)),
        ],
        out_specs=[
            pl.BlockSpec((tm, tn), lambda i, j: (i, j)),
            pl.BlockSpec((tm, V7X_LANES), lambda i, j: (i, 0)),
            pl.BlockSpec((tm // ATT_T, DIFF_V, ATT_T), lambda i, j: (i, 0, 0)),
        ],
        out_shape=[
            jax.ShapeDtypeStruct((S, PROJ_COLS), BF16),
            jax.ShapeDtypeStruct((S, V7X_LANES), F32),
            jax.ShapeDtypeStruct((S // ATT_T, DIFF_V, ATT_T), BF16),
        ],
        scratch_shapes=[pltpu.VMEM((tm, D_MODEL), BF16)],
        compiler_params=_params(("parallel", "arbitrary")),
        name="in_proj",
    )(x, g, w_main, w_ga, colscale)


def _log_sigmoid(x):
    return jnp.minimum(x, 0.0) - jnp.log(1.0 + jnp.exp(-jnp.abs(x)))


def _gla_kernel(q_ref, k_ref, v_ref, gate_ref, ga_ref, wup_ref, ba_ref, gn_ref,
                o_ref, st_scr, b_scr, a_scr):
    C, R = GLA_CHUNK, GLA_ROWS
    n_chunks = R // C

    @pl.when(pl.program_id(0) == 0)
    def _():
        st_scr[...] = jnp.zeros_like(st_scr)

    row = lax.broadcasted_iota(jnp.int32, (R, R), 0)
    col = lax.broadcasted_iota(jnp.int32, (R, R), 1)
    shift = C.bit_length() - 1
    in_chunk = (col <= row) & ((row >> shift) == (col >> shift))
    tril = jnp.where(in_chunk, 1.0, 0.0).astype(BF16)
    gn = gn_ref[...]

    z = jnp.dot(ga_ref[...].astype(BF16), wup_ref[...], preferred_element_type=F32) + ba_ref[...]
    la = _log_sigmoid(z) * (1.0 / GLA_TAU)
    la_hi = la.astype(BF16)
    la_lo = (la - la_hi.astype(F32)).astype(BF16)
    b_all = (jnp.dot(tril, la_hi, preferred_element_type=F32)
             + jnp.dot(tril, la_lo, preferred_element_type=F32))
    b_scr[...] = b_all
    b_first = jnp.concatenate(
        [jnp.broadcast_to(b_all[c * C:c * C + 1], (C, GLA_QK)) for c in range(n_chunks)], axis=0)
    span = jnp.max(b_first - b_all)
    small_span = span <= GLA_SAFE_SPAN

    def heads():
        for h in range(GLA_HEADS):
            ks = slice(h * GLA_DK, (h + 1) * GLA_DK)
            yield h, ks, q_ref[:, ks].astype(F32), k_ref[:, ks].astype(F32)

    @pl.when(small_span)
    def _():
        for h, ks, q, k in heads():
            d = b_first[:, ks] - b_scr[:, ks]
            q_t = (q * jnp.exp(-d)).astype(BF16)
            k_t = (k * jnp.exp(d)).astype(BF16)
            s = lax.dot_general(q_t, k_t, (((1,), (1,)), ((), ())),
                                preferred_element_type=F32)
            a_scr[h] = jnp.where(in_chunk, s, 0.0)

    @pl.when(jnp.logical_not(small_span))
    def _():
        a_scr[...] = jnp.zeros_like(a_scr)
        sub_row = lax.broadcasted_iota(jnp.int32, (GLA_SUB, C), 0)
        sub_col = lax.broadcasted_iota(jnp.int32, (GLA_SUB, C), 1)
        for h, ks, q_all, k_all in heads():
            for c in range(n_chunks):
                q = q_all[c * C:(c + 1) * C]
                k = k_all[c * C:(c + 1) * C]
                b = b_scr[c * C:(c + 1) * C, ks]
                for blk in range(C // GLA_SUB):
                    r0 = blk * GLA_SUB
                    b_blk = b[r0:r0 + GLA_SUB]
                    q_blk = q[r0:r0 + GLA_SUB]
                    b_ref0 = b[r0:r0 + 1]
                    a_blk = jnp.zeros((GLA_SUB, C), F32)
                    if blk > 0:
                        q_t = (q_blk * jnp.exp(b_blk - b_ref0)).astype(BF16)
                        k_t = (k * jnp.exp(jnp.minimum(b_ref0 - b, 0.0))).astype(BF16)
                        off = lax.dot_general(q_t, k_t, (((1,), (1,)), ((), ())),
                                              preferred_element_type=F32)
                        a_blk = jnp.where(sub_col < r0, off, 0.0)
                    for jj in range(GLA_SUB):
                        kj = k[r0 + jj:r0 + jj + 1]
                        bj = b[r0 + jj:r0 + jj + 1]
                        t = q_blk * kj * jnp.exp(jnp.minimum(b_blk - bj, 0.0))
                        cj = jnp.sum(t, axis=-1, keepdims=True)
                        a_blk = jnp.where((sub_col == r0 + jj) & (sub_row >= jj), cj, a_blk)
                    a_scr[h, c * C + r0:c * C + r0 + GLA_SUB, c * C:(c + 1) * C] = a_blk

    for h, ks, q, k in heads():
        vs = slice(h * GLA_DV, (h + 1) * GLA_DV)
        v = v_ref[:, vs]
        b = b_scr[:, ks]
        q_in = (q * jnp.exp(b)).astype(BF16)
        st = st_scr[h]
        o_inter = []
        for c in range(n_chunks):
            cr = slice(c * C, (c + 1) * C)
            o_inter.append(lax.dot_general(q_in[cr], st.astype(BF16), (((1,), (1,)), ((), ())),
                                           preferred_element_type=F32))
            b_last = b[(c + 1) * C - 1:(c + 1) * C]
            k_out = (k[cr] * jnp.exp(b_last - b[cr])).astype(BF16)
            upd = lax.dot_general(v[cr], k_out, (((0,), (0,)), ((), ())),
                                  preferred_element_type=F32)
            st = st * jnp.exp(b_last) + upd
        st_scr[h] = st
        o = (jnp.concatenate(o_inter, axis=0)
             + jnp.dot(a_scr[h].astype(BF16), v, preferred_element_type=F32))

        gate = gate_ref[:, vs].astype(F32)
        y = _rms(o, gn) * (gate * jax.nn.sigmoid(gate))
        o_ref[:, vs] = y.astype(BF16)


def _gla(proj, ga, wup, b_alpha, gla_norm):
    S = proj.shape[0]
    R = GLA_ROWS
    return pl.pallas_call(
        _gla_kernel,
        grid=(S // R,),
        in_specs=[
            pl.BlockSpec((R, GLA_QK), lambda i: (i, COL_GQ // GLA_QK)),
            pl.BlockSpec((R, GLA_QK), lambda i: (i, COL_GK // GLA_QK)),
            pl.BlockSpec((R, GLA_V), lambda i: (i, COL_GV // GLA_V)),
            pl.BlockSpec((R, GLA_V), lambda i: (i, COL_GG // GLA_V)),
            pl.BlockSpec((R, V7X_LANES), lambda i: (i, 0)),
            pl.BlockSpec((V7X_LANES, GLA_QK), lambda i: (0, 0)),
            pl.BlockSpec((1, GLA_QK), lambda i: (0, 0)),
            pl.BlockSpec((1, GLA_DV), lambda i: (0, 0)),
        ],
        out_specs=pl.BlockSpec((R, GLA_V), lambda i: (i, 0)),
        out_shape=jax.ShapeDtypeStruct((S, GLA_V), BF16),
        scratch_shapes=[
            pltpu.VMEM((GLA_HEADS, GLA_DV, GLA_DK), F32),
            pltpu.VMEM((R, GLA_QK), F32),
            pltpu.VMEM((GLA_HEADS, R, R), F32),
        ],
        compiler_params=_params(("arbitrary",)),
        name="gla",
    )(proj, proj, proj, proj, ga, wup, b_alpha, gla_norm)


def _attn_kernel(slopes_ref, lq1_ref, lk1_ref, lq2_ref, lk2_ref, gn_ref,
                 q_ref, k_ref, vt_ref, o_ref, feat_scr, kn_scr, qa_scr, s_scr, m_scr, l_scr,
                 acc_scr):
    TQ, TK, QW, D = q_ref.shape[0], ATT_T, ATT_QW, DIFF_DQK
    h = pl.program_id(0)
    i = pl.program_id(1)
    n_kb = TQ // TK
    lane = lax.broadcasted_iota(jnp.int32, (TK, V7X_LANES), 1)

    slope2 = slopes_ref[h] * LOG2E
    blk_lane = lax.broadcasted_iota(jnp.int32, (1, V7X_LANES), 1)

    @pl.when(i == 0)
    def _():
        row = lax.broadcasted_iota(jnp.int32, (TK, V7X_LANES), 0)

        def fill(blk, carry):
            r0 = pl.multiple_of(blk * TK, TK)
            b = slope2 * (row + r0).astype(F32)
            hi = b.astype(BF16).astype(F32)
            r1 = b - hi
            mid = r1.astype(BF16).astype(F32)
            lo = r1 - mid
            feat = jnp.where(lane == 0, hi,
                             jnp.where(lane == 1, mid, jnp.where(lane == 2, lo, 0.0)))
            feat_scr[pl.ds(r0, TK), :] = feat.astype(BF16)
            new = []
            for c in range(2):
                run, vec = carry[2 * c], carry[2 * c + 1]
                kc = k_ref[pl.ds(r0, TK), c * D:(c + 1) * D].astype(F32)
                sq = jnp.sum(kc * kc, axis=-1, keepdims=True)
                run = jnp.maximum(run, jnp.max(sq, axis=0, keepdims=True))
                new += [run, jnp.where(blk_lane == blk, jnp.sqrt(run), vec)]
            return tuple(new)

        zero = jnp.zeros((1, 1), F32)
        zvec = jnp.zeros((1, V7X_LANES), F32)
        done = lax.fori_loop(0, feat_scr.shape[0] // TK, fill, (zero, zvec, zero, zvec))
        kn_scr[0] = done[1]
        kn_scr[1] = done[3]

    m_scr[...] = jnp.full_like(m_scr, NEG_BIG)
    l_scr[...] = jnp.zeros_like(l_scr)
    acc_scr[...] = jnp.zeros_like(acc_scr)

    q_lane = lax.broadcasted_iota(jnp.int32, (TQ, V7X_LANES), 1)
    q_ones = jnp.where(q_lane < 3, 1.0, 0.0).astype(BF16)
    for c in range(2):
        qa_scr[c, :, 0:D] = q_ref[:, c * D:(c + 1) * D]
        qa_scr[c, :, D:2 * D] = q_ones
    key_row = lax.broadcasted_iota(jnp.int32, (TK, QW), 0)
    query_col = lax.broadcasted_iota(jnp.int32, (TK, QW), 1)

    def slabs(key_off):
        return [sl for sl in range(TQ // QW)
                if key_off is None or key_off <= sl * QW + QW - 1]

    def scores(j, slot, key_off):
        r0 = pl.multiple_of(j * TK, TK)
        feat = feat_scr[pl.ds(r0, TK), :]
        for c in range(2):
            k = jnp.concatenate([k_ref[pl.ds(r0, TK), c * D:(c + 1) * D], feat], axis=1)
            for sl in slabs(key_off):
                qs = slice(sl * QW, (sl + 1) * QW)
                s_scr[slot, c, :, qs] = lax.dot_general(
                    k, qa_scr[c, qs, :], (((1,), (1,)), ((), ())),
                    preferred_element_type=F32)

    def accumulate(j, slot, key_off):
        vt = vt_ref[j]
        for c in range(2):
            for sl in slabs(key_off):
                q_lo = sl * QW
                qs = slice(q_lo, q_lo + QW)
                s = s_scr[slot, c, :, qs]
                if key_off is not None and key_off + TK - 1 > q_lo:
                    s = jnp.where(key_row + key_off <= query_col + q_lo, s, NEG_BIG)
                m_prev = m_scr[c, :, qs]
                m_new = jnp.maximum(m_prev, jnp.max(s, axis=0, keepdims=True))
                alpha = jnp.exp2(m_prev - m_new)
                p = jnp.exp2(s - m_new)
                l_scr[c, :, qs] = alpha * l_scr[c, :, qs] + jnp.sum(p, axis=0, keepdims=True)
                acc_scr[c, :, qs] = alpha * acc_scr[c, :, qs] + jnp.dot(
                    vt, p.astype(BF16), preferred_element_type=F32)
                m_scr[c, :, qs] = m_new

    assert n_kb == 2
    scores(2 * i, 0, 0)
    scores(2 * i + 1, 1, TK)
    accumulate(2 * i + 1, 1, TK)
    scores(jnp.maximum(2 * i - 1, 0), 1, None)
    accumulate(2 * i, 0, 0)

    last_pos = (blk_lane * TK + (TK - 1)).astype(F32)
    skip = blk_lane < 2 * i
    for c in range(2):
        qf = q_ref[:, c * D:(c + 1) * D].astype(F32)
        qn = jnp.sqrt(jnp.max(jnp.sum(qf * qf, axis=-1, keepdims=True), axis=0, keepdims=True))
        bound = qn * kn_scr[c] * ATT_NORM_SLACK + slope2 * last_pos
        m_min = jnp.min(m_scr[c], axis=-1, keepdims=True)
        skip = skip & (bound < m_min - ATT_SKIP_LOG2)
    first_kept = jnp.min(jnp.where(skip, float(V7X_LANES), blk_lane.astype(F32)))
    n_skip = jnp.minimum(first_kept.astype(jnp.int32), 2 * i)
    n_pairs = i - n_skip // 2

    def body(t, carry):
        lo = 2 * (i - 1 - t)
        scores(lo, 0, None)
        accumulate(lo + 1, 1, None)
        scores(jnp.maximum(lo - 1, 0), 1, None)
        accumulate(lo, 0, None)
        return carry

    lax.fori_loop(0, n_pairs, body, 0)

    lam = (jnp.exp(jnp.sum(lq1_ref[...] * lk1_ref[...], axis=-1, keepdims=True))
           - jnp.exp(jnp.sum(lq2_ref[...] * lk2_ref[...], axis=-1, keepdims=True))
           + LAMBDA_INIT)
    o1 = acc_scr[0] / l_scr[0]
    o2 = acc_scr[1] / l_scr[1]
    o = (o1 - lam * o2).T
    o_ref[...] = (_rms(o, gn_ref[...]) * (1.0 - LAMBDA_INIT)).astype(BF16)


def _diff_attn(proj, vt, slopes, lq1, lk1, lq2, lk2, diff_norm):
    S = proj.shape[0]
    TQ = min(ATT_TQ, S)
    W = 2 * DIFF_DQK
    vec = pl.BlockSpec((1, DIFF_DQK), lambda h, i: (0, 0))
    return pl.pallas_call(
        _attn_kernel,
        grid=(DIFF_HEADS, S // TQ),
        in_specs=[
            pl.BlockSpec(memory_space=pltpu.SMEM),
            vec, vec, vec, vec,
            pl.BlockSpec((1, DIFF_DV), lambda h, i: (0, 0)),
            pl.BlockSpec((TQ, W), lambda h, i: (i, COL_DQ // W + h)),
            pl.BlockSpec((S, W), lambda h, i: (0, COL_DK // W + h)),
            pl.BlockSpec((S // ATT_T, DIFF_DV, ATT_T), lambda h, i: (0, h, 0)),
        ],
        out_specs=pl.BlockSpec((TQ, DIFF_DV), lambda h, i: (i, h)),
        out_shape=jax.ShapeDtypeStruct((S, DIFF_V), BF16),
        scratch_shapes=[
            pltpu.VMEM((S, V7X_LANES), BF16),
            pltpu.VMEM((2, 1, V7X_LANES), F32),
            pltpu.VMEM((2, TQ, W), BF16),
            pltpu.VMEM((2, 2, ATT_T, TQ), F32),
            pltpu.VMEM((2, 1, TQ), F32),
            pltpu.VMEM((2, 1, TQ), F32),
            pltpu.VMEM((2, DIFF_DV, TQ), F32),
        ],
        compiler_params=_params(("arbitrary", "arbitrary")),
        name="diff_attn",
    )(slopes, lq1, lk1, lq2, lk2, diff_norm, proj, proj, vt)


def _out_proj_kernel(oa_ref, ob_ref, wo_ref, x_ref, gpost_ref, gpre_ref,
                     x1_ref, h2_ref):
    for r0 in range(0, x_ref.shape[0], OUT_RH):
        rs = slice(r0, r0 + OUT_RH)
        m = (jnp.dot(oa_ref[rs, :], wo_ref[0:GLA_V, :], preferred_element_type=F32)
             + jnp.dot(ob_ref[rs, :], wo_ref[GLA_V:GLA_V + DIFF_V, :],
                       preferred_element_type=F32))
        x1 = x_ref[rs, :] + _rms(m, gpost_ref[...])
        x1_ref[rs, :] = x1
        h2_ref[rs, :] = _rms(x1, gpre_ref[...]).astype(BF16)


def _out_proj(o_a, o_b, w_o, x, g_post, g_pre):
    S = x.shape[0]
    tm = min(512, S)
    row = lambda i: (i, 0)
    fixed = lambda i: (0, 0)
    return pl.pallas_call(
        _out_proj_kernel,
        grid=(S // tm,),
        in_specs=[
            pl.BlockSpec((tm, GLA_V), row),
            pl.BlockSpec((tm, DIFF_V), row),
            pl.BlockSpec((GLA_V + DIFF_V, D_MODEL), fixed),
            pl.BlockSpec((tm, D_MODEL), row),
            pl.BlockSpec((1, D_MODEL), fixed),
            pl.BlockSpec((1, D_MODEL), fixed),
        ],
        out_specs=[
            pl.BlockSpec((tm, D_MODEL), row),
            pl.BlockSpec((tm, D_MODEL), row),
        ],
        out_shape=[
            jax.ShapeDtypeStruct((S, D_MODEL), F32),
            jax.ShapeDtypeStruct((S, D_MODEL), BF16),
        ],
        compiler_params=_params(("parallel",)),
        name="out_proj",
    )(o_a, o_b, w_o, x, g_post, g_pre)


def _gelu_tanh(x):
    k = -2.0 * math.sqrt(2.0 / math.pi) * LOG2E
    return x / (1.0 + jnp.exp2(x * (k + (k * 0.044715) * (x * x))))


def _ffn_kernel(h2_ref, wa_ref, wb_ref, cw_ref, cb_ref, wout_ref, x1_ref, gpost_ref,
                o_ref, acc_scr, abuf_scr, halo_scr, *, tm):
    i = pl.program_id(0)
    j = pl.program_id(1)
    nj = pl.num_programs(1)
    P = V7X_SUBLANES

    @pl.when((i == 0) & (j == 0))
    def _():
        acc_scr[...] = jnp.zeros_like(acc_scr)
        halo_scr[...] = jnp.zeros_like(halo_scr)

    abuf_scr[0:P, :] = halo_scr[j]
    for r0 in range(0, tm, FFN_RH):
        rs = slice(r0, r0 + FFN_RH)
        h2 = h2_ref[rs, :]
        contrib = None
        for c0 in range(0, wa_ref.shape[1], FFN_CW):
            cs = slice(c0, c0 + FFN_CW)
            a = jnp.dot(h2, wa_ref[:, cs], preferred_element_type=F32)
            b = jnp.dot(h2, wb_ref[:, cs], preferred_element_type=F32)
            abuf_scr[P + r0:P + r0 + FFN_RH, cs] = a
            a1 = abuf_scr[P - 1 + r0:P - 1 + r0 + FFN_RH, cs]
            a2 = abuf_scr[P - 2 + r0:P - 2 + r0 + FFN_RH, cs]
            conv = (cw_ref[0:1, cs] * a2 + cw_ref[1:2, cs] * a1 + cw_ref[2:3, cs] * a
                    + cb_ref[:, cs])
            g = (_gelu_tanh(conv) * b).astype(BF16)
            part = jnp.dot(g, wout_ref[cs, :], preferred_element_type=F32)
            contrib = part if contrib is None else contrib + part
        acc_scr[rs, :] += contrib
    halo_scr[j] = abuf_scr[tm:tm + P, :]

    @pl.when(j == nj - 1)
    def _():
        o_ref[...] = x1_ref[...] + _rms(acc_scr[...], gpost_ref[...])
        acc_scr[...] = jnp.zeros_like(acc_scr)


def _ffn(h2, w_ffn_in, conv_w, conv_b, w_ffn_out, x1, g_post):
    S = h2.shape[0]
    tm = min(512, S)
    tn = 512
    nj = D_FF // tn
    row = lambda i, j: (i, 0)
    return pl.pallas_call(
        functools.partial(_ffn_kernel, tm=tm),
        grid=(S // tm, nj),
        in_specs=[
            pl.BlockSpec((tm, D_MODEL), row),
            pl.BlockSpec((D_MODEL, tn), lambda i, j: (0, j)),
            pl.BlockSpec((D_MODEL, tn), lambda i, j: (0, nj + j)),
            pl.BlockSpec((CONV_W, tn), lambda i, j: (0, j)),
            pl.BlockSpec((1, tn), lambda i, j: (0, j)),
            pl.BlockSpec((tn, D_MODEL), lambda i, j: (j, 0)),
            pl.BlockSpec((tm, D_MODEL), row),
            pl.BlockSpec((1, D_MODEL), lambda i, j: (0, 0)),
        ],
        out_specs=pl.BlockSpec((tm, D_MODEL), row),
        out_shape=jax.ShapeDtypeStruct((S, D_MODEL), F32),
        scratch_shapes=[
            pltpu.VMEM((tm, D_MODEL), F32),
            pltpu.VMEM((tm + V7X_SUBLANES, tn), F32),
            pltpu.VMEM((nj, V7X_SUBLANES, tn), F32),
        ],
        compiler_params=_params(("arbitrary", "arbitrary")),
        name="ffn",
    )(h2, w_ffn_in, w_ffn_in, conv_w, conv_b, w_ffn_out, x1, g_post)


def _layer(x, attn_pre_norm, w_in, w_alpha_up, b_alpha, gla_norm, lambda_q1, lambda_k1,
           lambda_q2, lambda_k2, diff_norm, w_o, attn_post_norm, ffn_pre_norm, w_ffn_in,
           conv_w, conv_b, w_ffn_out, ffn_post_norm):
    vec = lambda p: p.reshape(1, -1).astype(F32)

    w_in16 = w_in.astype(BF16)
    w_main = jnp.concatenate(
        [w_in16[:, :GA_OFFSET], w_in16[:, GA_OFFSET + GLA_RANK:]], axis=1)
    w_ga = jnp.pad(w_in16[:, GA_OFFSET:GA_OFFSET + GLA_RANK],
                   ((0, 0), (0, V7X_LANES - GLA_RANK)))
    colscale = jnp.concatenate([
        jnp.full((GLA_QK,), GLA_DK ** -0.5, F32),
        jnp.ones((COL_DQ - COL_GK,), F32),
        jnp.full((DIFF_QK,), DIFF_DQK ** -0.5 * LOG2E, F32),
        jnp.ones((PROJ_COLS - COL_DK,), F32),
    ]).reshape(1, PROJ_COLS)
    wup = jnp.pad(w_alpha_up, ((0, V7X_LANES - GLA_RANK), (0, 0))).astype(BF16)
    slopes = jnp.asarray(
        [2.0 ** (-8.0 * (h + 1) / DIFF_HEADS) for h in range(DIFF_HEADS)], F32)

    proj, ga, vt = _in_proj(x, vec(attn_pre_norm), w_main, w_ga, colscale)
    o_a = _gla(proj, ga, wup, vec(b_alpha), vec(gla_norm))
    o_b = _diff_attn(proj, vt, slopes, vec(lambda_q1), vec(lambda_k1), vec(lambda_q2),
                     vec(lambda_k2), vec(diff_norm))
    x1, h2 = _out_proj(o_a, o_b, w_o.astype(BF16), x, vec(attn_post_norm),
                       vec(ffn_pre_norm))
    return _ffn(h2, w_ffn_in.astype(BF16), conv_w.astype(F32), vec(conv_b),
                w_ffn_out.astype(BF16), x1, vec(ffn_post_norm))


def kernel(x, attn_pre_norm, w_in, w_alpha_up, b_alpha, gla_norm, lambda_q1, lambda_k1,
           lambda_q2, lambda_k2, diff_norm, w_o, attn_post_norm, ffn_pre_norm, w_ffn_in,
           conv_w, conv_b, w_ffn_out, ffn_post_norm):
    B = x.shape[0]
    depth = w_in.shape[0]
    assert depth == 1, "lambda_init is baked for a single layer"
    outs = []
    for bi in range(B):
        xb = x[bi]
        for l in range(depth):
            xb = _layer(xb, attn_pre_norm[l], w_in[l], w_alpha_up[l], b_alpha[l], gla_norm[l],
                        lambda_q1[l], lambda_k1[l], lambda_q2[l], lambda_k2[l], diff_norm[l],
                        w_o[l], attn_post_norm[l], ffn_pre_norm[l], w_ffn_in[l], conv_w[l],
                        conv_b[l], w_ffn_out[l], ffn_post_norm[l])
        outs.append(xb)
    return outs[0][None] if B == 1 else jnp.stack(outs, axis=0)
```

```python
import functools
import math

import jax
import jax.numpy as jnp
from jax import lax
from jax.experimental import pallas as pl
from jax.experimental.pallas import tpu as pltpu

F32 = jnp.float32
BF16 = jnp.bfloat16

D_MODEL = 2048
GLA_HEADS = 4
GLA_DK = 128
GLA_DV = 256
GLA_RANK = 16
GLA_TAU = 16.0
DIFF_HEADS = 4
DIFF_DQK = 128
DIFF_DV = 256
D_FF = 5632
CONV_W = 3
EPS = 1e-6
LAMBDA_INIT = 0.8 - 0.6 * math.exp(-0.3 * 0)
LOG2E = math.log2(math.e)

GLA_QK = GLA_HEADS * GLA_DK
GLA_V = GLA_HEADS * GLA_DV
DIFF_QK = DIFF_HEADS * 2 * DIFF_DQK
DIFF_V = DIFF_HEADS * DIFF_DV

COL_GQ = 0
COL_GK = COL_GQ + GLA_QK
COL_GV = COL_GK + GLA_QK
COL_GG = COL_GV + GLA_V
COL_DQ = COL_GG + GLA_V
COL_DK = COL_DQ + DIFF_QK
COL_DV = COL_DK + DIFF_QK
PROJ_COLS = COL_DV + DIFF_V
GA_OFFSET = GLA_QK + GLA_QK + GLA_V + GLA_V

V7X_LANES = 128
V7X_SUBLANES = 8
V7X_VMEM_LIMIT_BYTES = 56 * 1024 * 1024

IN_TN = 1536
OUT_RH = 256
FFN_TN = 512
FFN_CW = 256
FFN_RH = 512
ATT_T = 512
ATT_TQ = 1024
ATT_QW = 256
ATT_SKIP_LOG2 = 160.0
ATT_NORM_SLACK = 1.001
GLA_CHUNK = 64
GLA_ROWS = 256
GLA_SUB = 16
GLA_SAFE_SPAN = 60.0
NEG_BIG = -1e30


def _rms(v, g):
    return v * lax.rsqrt(jnp.mean(v * v, axis=-1, keepdims=True) + EPS) * g


def _params(dims):
    return pltpu.CompilerParams(dimension_semantics=dims,
                                vmem_limit_bytes=V7X_VMEM_LIMIT_BYTES)


def _in_proj_kernel(x_ref, g_ref, w_ref, wga_ref, cs_ref, o_ref, ga_ref, vt_ref, h_scr,
                    *, tn):
    j = pl.program_id(1)

    @pl.when(j == 0)
    def _():
        h = _rms(x_ref[...], g_ref[...]).astype(BF16)
        h_scr[...] = h
        ga_ref[...] = jnp.dot(h, wga_ref[...], preferred_element_type=F32)

    acc = jnp.dot(h_scr[...], w_ref[...], preferred_element_type=F32)
    o_ref[...] = (acc * cs_ref[...]).astype(BF16)

    @pl.when(j == COL_DV // tn)
    def _():
        v0 = COL_DV % tn
        acc_t = acc[:, v0:v0 + DIFF_V].T.astype(BF16)
        for kb in range(vt_ref.shape[0]):
            vt_ref[kb] = acc_t[:, kb * ATT_T:(kb + 1) * ATT_T]


def _in_proj(x, g, w_main, w_ga, colscale):
    S = x.shape[0]
    tm = min(1024, S)
    tn = IN_TN
    return pl.pallas_call(
        functools.partial(_in_proj_kernel, tn=tn),
        grid=(S // tm, PROJ_COLS // tn),
        in_specs=[
            pl.BlockSpec((tm, D_MODEL), lambda i, j: (i, 0)),
            pl.BlockSpec((1, D_MODEL), lambda i, j: (0, 0)),
            pl.BlockSpec((None, D_MODEL, tn), lambda i, j: (j, 0, 0)),
            pl.BlockSpec((D_MODEL, V7X_LANES), lambda i, j: (0, 0)),
            pl.BlockSpec((1, tn), lambda i, j: (0, j)),
        ],
        out_specs=[
            pl.BlockSpec((tm, tn), lambda i, j: (i, j)),
            pl.BlockSpec((tm, V7X_LANES), lambda i, j: (i, 0)),
            pl.BlockSpec((tm // ATT_T, DIFF_V, ATT_T), lambda i, j: (i, 0, 0)),
        ],
        out_shape=[
            jax.ShapeDtypeStruct((S, PROJ_COLS), BF16),
            jax.ShapeDtypeStruct((S, V7X_LANES), F32),
            jax.ShapeDtypeStruct((S // ATT_T, DIFF_V, ATT_T), BF16),
        ],
        scratch_shapes=[pltpu.VMEM((tm, D_MODEL), BF16)],
        compiler_params=_params(("parallel", "arbitrary")),
        name="in_proj",
    )(x, g, w_main, w_ga, colscale)


def _log_sigmoid(x):
    return jnp.minimum(x, 0.0) - jnp.log(1.0 + jnp.exp(-jnp.abs(x)))


def _gla_kernel(q_ref, k_ref, v_ref, gate_ref, ga_ref, wup_ref, ba_ref, gn_ref,
                o_ref, st_scr, b_scr, a_scr):
    C, R = GLA_CHUNK, GLA_ROWS
    n_chunks = R // C

    @pl.when(pl.program_id(0) == 0)
    def _():
        st_scr[...] = jnp.zeros_like(st_scr)

    row = lax.broadcasted_iota(jnp.int32, (R, R), 0)
    col = lax.broadcasted_iota(jnp.int32, (R, R), 1)
    shift = C.bit_length() - 1
    in_chunk = (col <= row) & ((row >> shift) == (col >> shift))
    tril = jnp.where(in_chunk, 1.0, 0.0).astype(BF16)
    gn = gn_ref[...]

    z = jnp.dot(ga_ref[...].astype(BF16), wup_ref[...], preferred_element_type=F32) + ba_ref[...]
    la = _log_sigmoid(z) * (1.0 / GLA_TAU)
    la_hi = la.astype(BF16)
    la_lo = (la - la_hi.astype(F32)).astype(BF16)
    b_all = (jnp.dot(tril, la_hi, preferred_element_type=F32)
             + jnp.dot(tril, la_lo, preferred_element_type=F32))
    b_scr[...] = b_all
    b_first = jnp.concatenate(
        [jnp.broadcast_to(b_all[c * C:c * C + 1], (C, GLA_QK)) for c in range(n_chunks)], axis=0)
    span = jnp.max(b_first - b_all)
    small_span = span <= GLA_SAFE_SPAN

    def heads():
        for h in range(GLA_HEADS):
            ks = slice(h * GLA_DK, (h + 1) * GLA_DK)
            yield h, ks, q_ref[:, ks].astype(F32), k_ref[:, ks].astype(F32)

    @pl.when(small_span)
    def _():
        for h, ks, q, k in heads():
            d = b_first[:, ks] - b_scr[:, ks]
            q_t = (q * jnp.exp(-d)).astype(BF16)
            k_t = (k * jnp.exp(d)).astype(BF16)
            s = lax.dot_general(q_t, k_t, (((1,), (1,)), ((), ())),
                                preferred_element_type=F32)
            a_scr[h] = jnp.where(in_chunk, s, 0.0)

    @pl.when(jnp.logical_not(small_span))
    def _():
        a_scr[...] = jnp.zeros_like(a_scr)
        sub_row = lax.broadcasted_iota(jnp.int32, (GLA_SUB, C), 0)
        sub_col = lax.broadcasted_iota(jnp.int32, (GLA_SUB, C), 1)
        for h, ks, q_all, k_all in heads():
            for c in range(n_chunks):
                q = q_all[c * C:(c + 1) * C]
                k = k_all[c * C:(c + 1) * C]
                b = b_scr[c * C:(c + 1) * C, ks]
                for blk in range(C // GLA_SUB):
                    r0 = blk * GLA_SUB
                    b_blk = b[r0:r0 + GLA_SUB]
                    q_blk = q[r0:r0 + GLA_SUB]
                    b_ref0 = b[r0:r0 + 1]
                    a_blk = jnp.zeros((GLA_SUB, C), F32)
                    if blk > 0:
                        q_t = (q_blk * jnp.exp(b_blk - b_ref0)).astype(BF16)
                        k_t = (k * jnp.exp(jnp.minimum(b_ref0 - b, 0.0))).astype(BF16)
                        off = lax.dot_general(q_t, k_t, (((1,), (1,)), ((), ())),
                                              preferred_element_type=F32)
                        a_blk = jnp.where(sub_col < r0, off, 0.0)
                    for jj in range(GLA_SUB):
                        kj = k[r0 + jj:r0 + jj + 1]
                        bj = b[r0 + jj:r0 + jj + 1]
                        t = q_blk * kj * jnp.exp(jnp.minimum(b_blk - bj, 0.0))
                        cj = jnp.sum(t, axis=-1, keepdims=True)
                        a_blk = jnp.where((sub_col == r0 + jj) & (sub_row >= jj), cj, a_blk)
                    a_scr[h, c * C + r0:c * C + r0 + GLA_SUB, c * C:(c + 1) * C] = a_blk

    for h, ks, q, k in heads():
        vs = slice(h * GLA_DV, (h + 1) * GLA_DV)
        v = v_ref[:, vs]
        b = b_scr[:, ks]
        q_in = (q * jnp.exp(b)).astype(BF16)
        st = st_scr[h]
        o_inter = []
        for c in range(n_chunks):
            cr = slice(c * C, (c + 1) * C)
            o_inter.append(lax.dot_general(q_in[cr], st.astype(BF16), (((1,), (1,)), ((), ())),
                                           preferred_element_type=F32))
            b_last = b[(c + 1) * C - 1:(c + 1) * C]
            k_out = (k[cr] * jnp.exp(b_last - b[cr])).astype(BF16)
            upd = lax.dot_general(v[cr], k_out, (((0,), (0,)), ((), ())),
                                  preferred_element_type=F32)
            st = st * jnp.exp(b_last) + upd
        st_scr[h] = st
        o = (jnp.concatenate(o_inter, axis=0)
             + jnp.dot(a_scr[h].astype(BF16), v, preferred_element_type=F32))

        gate = gate_ref[:, vs].astype(F32)
        y = _rms(o, gn) * (gate * jax.nn.sigmoid(gate))
        o_ref[:, vs] = y.astype(BF16)


def _gla(proj, ga, wup, b_alpha, gla_norm):
    S = proj.shape[0]
    R = GLA_ROWS
    return pl.pallas_call(
        _gla_kernel,
        grid=(S // R,),
        in_specs=[
            pl.BlockSpec((R, GLA_QK), lambda i: (i, COL_GQ // GLA_QK)),
            pl.BlockSpec((R, GLA_QK), lambda i: (i, COL_GK // GLA_QK)),
            pl.BlockSpec((R, GLA_V), lambda i: (i, COL_GV // GLA_V)),
            pl.BlockSpec((R, GLA_V), lambda i: (i, COL_GG // GLA_V)),
            pl.BlockSpec((R, V7X_LANES), lambda i: (i, 0)),
            pl.BlockSpec((V7X_LANES, GLA_QK), lambda i: (0, 0)),
            pl.BlockSpec((1, GLA_QK), lambda i: (0, 0)),
            pl.BlockSpec((1, GLA_DV), lambda i: (0, 0)),
        ],
        out_specs=pl.BlockSpec((R, GLA_V), lambda i: (i, 0)),
        out_shape=jax.ShapeDtypeStruct((S, GLA_V), BF16),
        scratch_shapes=[
            pltpu.VMEM((GLA_HEADS, GLA_DV, GLA_DK), F32),
            pltpu.VMEM((R, GLA_QK), F32),
            pltpu.VMEM((GLA_HEADS, R, R), F32),
        ],
        compiler_params=_params(("arbitrary",)),
        name="gla",
    )(proj, proj, proj, proj, ga, wup, b_alpha, gla_norm)


def _attn_kernel(slopes_ref, lq1_ref, lk1_ref, lq2_ref, lk2_ref, gn_ref,
                 q_ref, k_ref, vt_ref, o_ref, feat_scr, kn_scr, qa_scr, s_scr, m_scr, l_scr,
                 acc_scr):
    TQ, TK, QW, D = q_ref.shape[0], ATT_T, ATT_QW, DIFF_DQK
    h = pl.program_id(0)
    i = pl.program_id(1)
    n_kb = TQ // TK
    lane = lax.broadcasted_iota(jnp.int32, (TK, V7X_LANES), 1)

    slope2 = slopes_ref[h] * LOG2E
    blk_lane = lax.broadcasted_iota(jnp.int32, (1, V7X_LANES), 1)

    @pl.when(i == 0)
    def _():
        row = lax.broadcasted_iota(jnp.int32, (TK, V7X_LANES), 0)

        def fill(blk, carry):
            r0 = pl.multiple_of(blk * TK, TK)
            b = slope2 * (row + r0).astype(F32)
            hi = b.astype(BF16).astype(F32)
            r1 = b - hi
            mid = r1.astype(BF16).astype(F32)
            lo = r1 - mid
            feat = jnp.where(lane == 0, hi,
                             jnp.where(lane == 1, mid, jnp.where(lane == 2, lo, 0.0)))
            feat_scr[pl.ds(r0, TK), :] = feat.astype(BF16)
            new = []
            for c in range(2):
                run, vec = carry[2 * c], carry[2 * c + 1]
                kc = k_ref[pl.ds(r0, TK), c * D:(c + 1) * D].astype(F32)
                sq = jnp.sum(kc * kc, axis=-1, keepdims=True)
                run = jnp.maximum(run, jnp.max(sq, axis=0, keepdims=True))
                new += [run, jnp.where(blk_lane == blk, jnp.sqrt(run), vec)]
            return tuple(new)

        zero = jnp.zeros((1, 1), F32)
        zvec = jnp.zeros((1, V7X_LANES), F32)
        done = lax.fori_loop(0, feat_scr.shape[0] // TK, fill, (zero, zvec, zero, zvec))
        kn_scr[0] = done[1]
        kn_scr[1] = done[3]

    m_scr[...] = jnp.full_like(m_scr, NEG_BIG)
    l_scr[...] = jnp.zeros_like(l_scr)
    acc_scr[...] = jnp.zeros_like(acc_scr)

    q_lane = lax.broadcasted_iota(jnp.int32, (TQ, V7X_LANES), 1)
    q_ones = jnp.where(q_lane < 3, 1.0, 0.0).astype(BF16)
    for c in range(2):
        qa_scr[c, :, 0:D] = q_ref[:, c * D:(c + 1) * D]
        qa_scr[c, :, D:2 * D] = q_ones
    key_row = lax.broadcasted_iota(jnp.int32, (TK, QW), 0)
    query_col = lax.broadcasted_iota(jnp.int32, (TK, QW), 1)

    def slabs(key_off):
        return [sl for sl in range(TQ // QW)
                if key_off is None or key_off <= sl * QW + QW - 1]

    def scores(j, slot, key_off):
        r0 = pl.multiple_of(j * TK, TK)
        feat = feat_scr[pl.ds(r0, TK), :]
        for c in range(2):
            k = jnp.concatenate([k_ref[pl.ds(r0, TK), c * D:(c + 1) * D], feat], axis=1)
            for sl in slabs(key_off):
                qs = slice(sl * QW, (sl + 1) * QW)
                s_scr[slot, c, :, qs] = lax.dot_general(
                    k, qa_scr[c, qs, :], (((1,), (1,)), ((), ())),
                    preferred_element_type=F32)

    def accumulate(j, slot, key_off):
        vt = vt_ref[j]
        for c in range(2):
            for sl in slabs(key_off):
                q_lo = sl * QW
                qs = slice(q_lo, q_lo + QW)
                s = s_scr[slot, c, :, qs]
                if key_off is not None and key_off + TK - 1 > q_lo:
                    s = jnp.where(key_row + key_off <= query_col + q_lo, s, NEG_BIG)
                m_prev = m_scr[c, :, qs]
                m_new = jnp.maximum(m_prev, jnp.max(s, axis=0, keepdims=True))
                alpha = jnp.exp2(m_prev - m_new)
                p = jnp.exp2(s - m_new)
                l_scr[c, :, qs] = alpha * l_scr[c, :, qs] + jnp.sum(p, axis=0, keepdims=True)
                acc_scr[c, :, qs] = alpha * acc_scr[c, :, qs] + jnp.dot(
                    vt, p.astype(BF16), preferred_element_type=F32)
                m_scr[c, :, qs] = m_new

    assert n_kb == 2
    scores(2 * i, 0, 0)
    scores(2 * i + 1, 1, TK)
    accumulate(2 * i + 1, 1, TK)
    scores(jnp.maximum(2 * i - 1, 0), 1, None)
    accumulate(2 * i, 0, 0)

    last_pos = (blk_lane * TK + (TK - 1)).astype(F32)
    skip = blk_lane < 2 * i
    for c in range(2):
        qf = q_ref[:, c * D:(c + 1) * D].astype(F32)
        qn = jnp.sqrt(jnp.max(jnp.sum(qf * qf, axis=-1, keepdims=True), axis=0, keepdims=True))
        bound = qn * kn_scr[c] * ATT_NORM_SLACK + slope2 * last_pos
        m_min = jnp.min(m_scr[c], axis=-1, keepdims=True)
        skip = skip & (bound < m_min - ATT_SKIP_LOG2)
    first_kept = jnp.min(jnp.where(skip, float(V7X_LANES), blk_lane.astype(F32)))
    n_skip = jnp.minimum(first_kept.astype(jnp.int32), 2 * i)
    n_pairs = i - n_skip // 2

    def body(t, carry):
        lo = 2 * (i - 1 - t)
        scores(lo, 0, None)
        accumulate(lo + 1, 1, None)
        scores(jnp.maximum(lo - 1, 0), 1, None)
        accumulate(lo, 0, None)
        return carry

    lax.fori_loop(0, n_pairs, body, 0)

    lam = (jnp.exp(jnp.sum(lq1_ref[...] * lk1_ref[...], axis=-1, keepdims=True))
           - jnp.exp(jnp.sum(lq2_ref[...] * lk2_ref[...], axis=-1, keepdims=True))
           + LAMBDA_INIT)
    o1 = acc_scr[0] / l_scr[0]
    o2 = acc_scr[1] / l_scr[1]
    o = (o1 - lam * o2).T
    o_ref[...] = (_rms(o, gn_ref[...]) * (1.0 - LAMBDA_INIT)).astype(BF16)


def _diff_attn(proj, vt, slopes, lq1, lk1, lq2, lk2, diff_norm):
    S = proj.shape[0]
    TQ = min(ATT_TQ, S)
    W = 2 * DIFF_DQK
    vec = pl.BlockSpec((1, DIFF_DQK), lambda h, i: (0, 0))
    return pl.pallas_call(
        _attn_kernel,
        grid=(DIFF_HEADS, S // TQ),
        in_specs=[
            pl.BlockSpec(memory_space=pltpu.SMEM),
            vec, vec, vec, vec,
            pl.BlockSpec((1, DIFF_DV), lambda h, i: (0, 0)),
            pl.BlockSpec((TQ, W), lambda h, i: (i, COL_DQ // W + h)),
            pl.BlockSpec((S, W), lambda h, i: (0, COL_DK // W + h)),
            pl.BlockSpec((S // ATT_T, DIFF_DV, ATT_T), lambda h, i: (0, h, 0)),
        ],
        out_specs=pl.BlockSpec((TQ, DIFF_DV), lambda h, i: (i, h)),
        out_shape=jax.ShapeDtypeStruct((S, DIFF_V), BF16),
        scratch_shapes=[
            pltpu.VMEM((S, V7X_LANES), BF16),
            pltpu.VMEM((2, 1, V7X_LANES), F32),
            pltpu.VMEM((2, TQ, W), BF16),
            pltpu.VMEM((2, 2, ATT_T, TQ), F32),
            pltpu.VMEM((2, 1, TQ), F32),
            pltpu.VMEM((2, 1, TQ), F32),
            pltpu.VMEM((2, DIFF_DV, TQ), F32),
        ],
        compiler_params=_params(("arbitrary", "arbitrary")),
        name="diff_attn",
    )(slopes, lq1, lk1, lq2, lk2, diff_norm, proj, proj, vt)


def _out_proj_kernel(oa_ref, ob_ref, wo_ref, x_ref, gpost_ref, gpre_ref,
                     x1_ref, h2_ref):
    for r0 in range(0, x_ref.shape[0], OUT_RH):
        rs = slice(r0, r0 + OUT_RH)
        m = (jnp.dot(oa_ref[rs, :], wo_ref[0:GLA_V, :], preferred_element_type=F32)
             + jnp.dot(ob_ref[rs, :], wo_ref[GLA_V:GLA_V + DIFF_V, :],
                       preferred_element_type=F32))
        x1 = x_ref[rs, :] + _rms(m, gpost_ref[...])
        x1_ref[rs, :] = x1
        h2_ref[rs, :] = _rms(x1, gpre_ref[...]).astype(BF16)


def _out_proj(o_a, o_b, w_o, x, g_post, g_pre):
    S = x.shape[0]
    tm = min(512, S)
    row = lambda i: (i, 0)
    fixed = lambda i: (0, 0)
    return pl.pallas_call(
        _out_proj_kernel,
        grid=(S // tm,),
        in_specs=[
            pl.BlockSpec((tm, GLA_V), row),
            pl.BlockSpec((tm, DIFF_V), row),
            pl.BlockSpec((GLA_V + DIFF_V, D_MODEL), fixed),
            pl.BlockSpec((tm, D_MODEL), row),
            pl.BlockSpec((1, D_MODEL), fixed),
            pl.BlockSpec((1, D_MODEL), fixed),
        ],
        out_specs=[
            pl.BlockSpec((tm, D_MODEL), row),
            pl.BlockSpec((tm, D_MODEL), row),
        ],
        out_shape=[
            jax.ShapeDtypeStruct((S, D_MODEL), F32),
            jax.ShapeDtypeStruct((S, D_MODEL), BF16),
        ],
        compiler_params=_params(("parallel",)),
        name="out_proj",
    )(o_a, o_b, w_o, x, g_post, g_pre)


def _gelu_tanh(x):
    k = -2.0 * math.sqrt(2.0 / math.pi) * LOG2E
    return x / (1.0 + jnp.exp2(x * (k + (k * 0.044715) * (x * x))))


def _ffn_kernel(h2_ref, wa_ref, wb_ref, cw_ref, cb_ref, wout_ref, x1_ref, gpost_ref,
                o_ref, acc_scr, abuf_scr, halo_scr, *, tm):
    i = pl.program_id(0)
    j = pl.program_id(1)
    nj = pl.num_programs(1)
    P = V7X_SUBLANES

    @pl.when((i == 0) & (j == 0))
    def _():
        acc_scr[...] = jnp.zeros_like(acc_scr)
        halo_scr[...] = jnp.zeros_like(halo_scr)

    abuf_scr[0:P, :] = halo_scr[j]
    for r0 in range(0, tm, FFN_RH):
        rs = slice(r0, r0 + FFN_RH)
        h2 = h2_ref[rs, :]
        contrib = None
        for c0 in range(0, wa_ref.shape[1], FFN_CW):
            cs = slice(c0, c0 + FFN_CW)
            a = jnp.dot(h2, wa_ref[:, cs], preferred_element_type=F32)
            b = jnp.dot(h2, wb_ref[:, cs], preferred_element_type=F32)
            abuf_scr[P + r0:P + r0 + FFN_RH, cs] = a
            a1 = abuf_scr[P - 1 + r0:P - 1 + r0 + FFN_RH, cs]
            a2 = abuf_scr[P - 2 + r0:P - 2 + r0 + FFN_RH, cs]
            conv = (cw_ref[0:1, cs] * a2 + cw_ref[1:2, cs] * a1 + cw_ref[2:3, cs] * a
                    + cb_ref[:, cs])
            g = (_gelu_tanh(conv) * b).astype(BF16)
            part = jnp.dot(g, wout_ref[cs, :], preferred_element_type=F32)
            contrib = part if contrib is None else contrib + part
        acc_scr[rs, :] += contrib
    halo_scr[j] = abuf_scr[tm:tm + P, :]

    @pl.when(j == nj - 1)
    def _():
        o_ref[...] = x1_ref[...] + _rms(acc_scr[...], gpost_ref[...])
        acc_scr[...] = jnp.zeros_like(acc_scr)


def _ffn(h2, w_ffn_in, conv_w, conv_b, w_ffn_out, x1, g_post):
    S = h2.shape[0]
    tm = min(512, S)
    tn = FFN_TN
    nj = D_FF // tn
    row = lambda i, j: (i, 0)
    return pl.pallas_call(
        functools.partial(_ffn_kernel, tm=tm),
        grid=(S // tm, nj),
        in_specs=[
            pl.BlockSpec((tm, D_MODEL), row),
            pl.BlockSpec((None, None, D_MODEL, tn), lambda i, j: (0, j, 0, 0)),
            pl.BlockSpec((None, None, D_MODEL, tn), lambda i, j: (1, j, 0, 0)),
            pl.BlockSpec((CONV_W, tn), lambda i, j: (0, j)),
            pl.BlockSpec((1, tn), lambda i, j: (0, j)),
            pl.BlockSpec((tn, D_MODEL), lambda i, j: (j, 0)),
            pl.BlockSpec((tm, D_MODEL), row),
            pl.BlockSpec((1, D_MODEL), lambda i, j: (0, 0)),
        ],
        out_specs=pl.BlockSpec((tm, D_MODEL), row),
        out_shape=jax.ShapeDtypeStruct((S, D_MODEL), F32),
        scratch_shapes=[
            pltpu.VMEM((tm, D_MODEL), F32),
            pltpu.VMEM((tm + V7X_SUBLANES, tn), F32),
            pltpu.VMEM((nj, V7X_SUBLANES, tn), F32),
        ],
        compiler_params=_params(("arbitrary", "arbitrary")),
        name="ffn",
    )(h2, w_ffn_in, w_ffn_in, conv_w, conv_b, w_ffn_out, x1, g_post)


def _layer(x, attn_pre_norm, w_in, w_alpha_up, b_alpha, gla_norm, lambda_q1, lambda_k1,
           lambda_q2, lambda_k2, diff_norm, w_o, attn_post_norm, ffn_pre_norm, w_ffn_in,
           conv_w, conv_b, w_ffn_out, ffn_post_norm):
    vec = lambda p: p.reshape(1, -1).astype(F32)

    w_in16 = lax.optimization_barrier(w_in.astype(BF16))
    w_main = jnp.concatenate(
        [w_in16[:, :GA_OFFSET], w_in16[:, GA_OFFSET + GLA_RANK:]], axis=1
    ).reshape(D_MODEL, PROJ_COLS // IN_TN, IN_TN).transpose(1, 0, 2)
    w_ga = jnp.pad(w_in16[:, GA_OFFSET:GA_OFFSET + GLA_RANK],
                   ((0, 0), (0, V7X_LANES - GLA_RANK)))
    colscale = jnp.concatenate([
        jnp.full((GLA_QK,), GLA_DK ** -0.5, F32),
        jnp.ones((COL_DQ - COL_GK,), F32),
        jnp.full((DIFF_QK,), DIFF_DQK ** -0.5 * LOG2E, F32),
        jnp.ones((PROJ_COLS - COL_DK,), F32),
    ]).reshape(1, PROJ_COLS)
    wup = jnp.pad(w_alpha_up, ((0, V7X_LANES - GLA_RANK), (0, 0))).astype(BF16)
    slopes = jnp.asarray(
        [2.0 ** (-8.0 * (h + 1) / DIFF_HEADS) for h in range(DIFF_HEADS)], F32)

    proj, ga, vt = _in_proj(x, vec(attn_pre_norm), w_main, w_ga, colscale)
    o_a = _gla(proj, ga, wup, vec(b_alpha), vec(gla_norm))
    o_b = _diff_attn(proj, vt, slopes, vec(lambda_q1), vec(lambda_k1), vec(lambda_q2),
                     vec(lambda_k2), vec(diff_norm))
    x1, h2 = _out_proj(o_a, o_b, w_o.astype(BF16), x, vec(attn_post_norm),
                       vec(ffn_pre_norm))
    w_ab = w_ffn_in.astype(BF16).reshape(D_MODEL, 2, D_FF // FFN_TN, FFN_TN).transpose(1, 2, 0, 3)
    return _ffn(h2, w_ab, conv_w.astype(F32), vec(conv_b),
                w_ffn_out.astype(BF16), x1, vec(ffn_post_norm))


def kernel(x, attn_pre_norm, w_in, w_alpha_up, b_alpha, gla_norm, lambda_q1, lambda_k1,
           lambda_q2, lambda_k2, diff_norm, w_o, attn_post_norm, ffn_pre_norm, w_ffn_in,
           conv_w, conv_b, w_ffn_out, ffn_post_norm):
    B = x.shape[0]
    depth = w_in.shape[0]
    assert depth == 1, "lambda_init is baked for a single layer"
    outs = []
    for bi in range(B):
        xb = x[bi]
        for l in range(depth):
            xb = _layer(xb, attn_pre_norm[l], w_in[l], w_alpha_up[l], b_alpha[l], gla_norm[l],
                        lambda_q1[l], lambda_k1[l], lambda_q2[l], lambda_k2[l], diff_norm[l],
                        w_o[l], attn_post_norm[l], ffn_pre_norm[l], w_ffn_in[l], conv_w[l],
                        conv_b[l], w_ffn_out[l], ffn_post_norm[l])
        outs.append(xb)
    return outs[0][None] if B == 1 else jnp.stack(outs, axis=0)
```

```python
import functools
import math

import jax
import jax.numpy as jnp
from jax import lax
from jax.experimental import pallas as pl
from jax.experimental.pallas import tpu as pltpu

F32 = jnp.float32
BF16 = jnp.bfloat16

D_MODEL = 2048
GLA_HEADS = 4
GLA_DK = 128
GLA_DV = 256
GLA_RANK = 16
GLA_TAU = 16.0
DIFF_HEADS = 4
DIFF_DQK = 128
DIFF_DV = 256
D_FF = 5632
CONV_W = 3
EPS = 1e-6
LAMBDA_INIT = 0.8 - 0.6 * math.exp(-0.3 * 0)
LOG2E = math.log2(math.e)

GLA_QK = GLA_HEADS * GLA_DK
GLA_V = GLA_HEADS * GLA_DV
DIFF_QK = DIFF_HEADS * 2 * DIFF_DQK
DIFF_V = DIFF_HEADS * DIFF_DV

COL_GQ = 0
COL_GK = COL_GQ + GLA_QK
COL_GV = COL_GK + GLA_QK
COL_GG = COL_GV + GLA_V
COL_DQ = COL_GG + GLA_V
COL_DK = COL_DQ + DIFF_QK
COL_DV = COL_DK + DIFF_QK
PROJ_COLS = COL_DV + DIFF_V
GA_OFFSET = GLA_QK + GLA_QK + GLA_V + GLA_V

V7X_LANES = 128
V7X_SUBLANES = 8
V7X_VMEM_LIMIT_BYTES = 56 * 1024 * 1024

IN_TN = 1536
OUT_RH = 256
FFN_CW = 512
FFN_RH = 512
ATT_T = 512
ATT_TQ = 1024
ATT_QW = 256
ATT_SKIP_LOG2 = 160.0
ATT_NORM_SLACK = 1.001
GLA_CHUNK = 64
GLA_ROWS = 256
GLA_SUB = 16
GLA_SAFE_SPAN = 60.0
NEG_BIG = -1e30


def _rms(v, g):
    return v * lax.rsqrt(jnp.mean(v * v, axis=-1, keepdims=True) + EPS) * g


def _params(dims):
    return pltpu.CompilerParams(dimension_semantics=dims,
                                vmem_limit_bytes=V7X_VMEM_LIMIT_BYTES)


def _in_proj_kernel(x_ref, g_ref, w_ref, wga_ref, cs_ref, o_ref, ga_ref, vt_ref, h_scr,
                    *, tn):
    j = pl.program_id(1)

    @pl.when(j == 0)
    def _():
        h = _rms(x_ref[...], g_ref[...]).astype(BF16)
        h_scr[...] = h
        ga_ref[...] = jnp.dot(h, wga_ref[...], preferred_element_type=F32)

    acc = jnp.dot(h_scr[...], w_ref[...], preferred_element_type=F32)
    o_ref[...] = (acc * cs_ref[...]).astype(BF16)

    @pl.when(j == COL_DV // tn)
    def _():
        v0 = COL_DV % tn
        acc_t = acc[:, v0:v0 + DIFF_V].T.astype(BF16)
        for kb in range(vt_ref.shape[0]):
            vt_ref[kb] = acc_t[:, kb * ATT_T:(kb + 1) * ATT_T]


def _in_proj(x, g, w_main, w_ga, colscale):
    S = x.shape[0]
    tm = min(1024, S)
    tn = IN_TN
    return pl.pallas_call(
        functools.partial(_in_proj_kernel, tn=tn),
        grid=(S // tm, PROJ_COLS // tn),
        in_specs=[
            pl.BlockSpec((tm, D_MODEL), lambda i, j: (i, 0)),
            pl.BlockSpec((1, D_MODEL), lambda i, j: (0, 0)),
            pl.BlockSpec((D_MODEL, tn), lambda i, j: (0, j)),
            pl.BlockSpec((D_MODEL, V7X_LANES), lambda i, j: (0, 0)),
            pl.BlockSpec((1, tn), lambda i, j: (0, j)),
        ],
        out_specs=[
            pl.BlockSpec((tm, tn), lambda i, j: (i, j)),
            pl.BlockSpec((tm, V7X_LANES), lambda i, j: (i, 0)),
            pl.BlockSpec((tm // ATT_T, DIFF_V, ATT_T), lambda i, j: (i, 0, 0)),
        ],
        out_shape=[
            jax.ShapeDtypeStruct((S, PROJ_COLS), BF16),
            jax.ShapeDtypeStruct((S, V7X_LANES), F32),
            jax.ShapeDtypeStruct((S // ATT_T, DIFF_V, ATT_T), BF16),
        ],
        scratch_shapes=[pltpu.VMEM((tm, D_MODEL), BF16)],
        compiler_params=_params(("parallel", "arbitrary")),
        name="in_proj",
    )(x, g, w_main, w_ga, colscale)


def _log_sigmoid(x):
    return jnp.minimum(x, 0.0) - jnp.log(1.0 + jnp.exp(-jnp.abs(x)))


def _gla_kernel(q_ref, k_ref, v_ref, gate_ref, ga_ref, wup_ref, ba_ref, gn_ref,
                o_ref, st_scr, b_scr, a_scr):
    C, R = GLA_CHUNK, GLA_ROWS
    n_chunks = R // C

    @pl.when(pl.program_id(0) == 0)
    def _():
        st_scr[...] = jnp.zeros_like(st_scr)

    row = lax.broadcasted_iota(jnp.int32, (R, R), 0)
    col = lax.broadcasted_iota(jnp.int32, (R, R), 1)
    shift = C.bit_length() - 1
    in_chunk = (col <= row) & ((row >> shift) == (col >> shift))
    tril = jnp.where(in_chunk, 1.0, 0.0).astype(BF16)
    gn = gn_ref[...]

    z = jnp.dot(ga_ref[...].astype(BF16), wup_ref[...], preferred_element_type=F32) + ba_ref[...]
    la = _log_sigmoid(z) * (1.0 / GLA_TAU)
    la_hi = la.astype(BF16)
    la_lo = (la - la_hi.astype(F32)).astype(BF16)
    b_all = (jnp.dot(tril, la_hi, preferred_element_type=F32)
             + jnp.dot(tril, la_lo, preferred_element_type=F32))
    b_scr[...] = b_all
    b_first = jnp.concatenate(
        [jnp.broadcast_to(b_all[c * C:c * C + 1], (C, GLA_QK)) for c in range(n_chunks)], axis=0)
    span = jnp.max(b_first - b_all)
    small_span = span <= GLA_SAFE_SPAN

    def heads():
        for h in range(GLA_HEADS):
            ks = slice(h * GLA_DK, (h + 1) * GLA_DK)
            yield h, ks, q_ref[:, ks].astype(F32), k_ref[:, ks].astype(F32)

    @pl.when(small_span)
    def _():
        for h, ks, q, k in heads():
            d = b_first[:, ks] - b_scr[:, ks]
            q_t = (q * jnp.exp(-d)).astype(BF16)
            k_t = (k * jnp.exp(d)).astype(BF16)
            s = lax.dot_general(q_t, k_t, (((1,), (1,)), ((), ())),
                                preferred_element_type=F32)
            a_scr[h] = jnp.where(in_chunk, s, 0.0)

    @pl.when(jnp.logical_not(small_span))
    def _():
        a_scr[...] = jnp.zeros_like(a_scr)
        sub_row = lax.broadcasted_iota(jnp.int32, (GLA_SUB, C), 0)
        sub_col = lax.broadcasted_iota(jnp.int32, (GLA_SUB, C), 1)
        for h, ks, q_all, k_all in heads():
            for c in range(n_chunks):
                q = q_all[c * C:(c + 1) * C]
                k = k_all[c * C:(c + 1) * C]
                b = b_scr[c * C:(c + 1) * C, ks]
                for blk in range(C // GLA_SUB):
                    r0 = blk * GLA_SUB
                    b_blk = b[r0:r0 + GLA_SUB]
                    q_blk = q[r0:r0 + GLA_SUB]
                    b_ref0 = b[r0:r0 + 1]
                    a_blk = jnp.zeros((GLA_SUB, C), F32)
                    if blk > 0:
                        q_t = (q_blk * jnp.exp(b_blk - b_ref0)).astype(BF16)
                        k_t = (k * jnp.exp(jnp.minimum(b_ref0 - b, 0.0))).astype(BF16)
                        off = lax.dot_general(q_t, k_t, (((1,), (1,)), ((), ())),
                                              preferred_element_type=F32)
                        a_blk = jnp.where(sub_col < r0, off, 0.0)
                    for jj in range(GLA_SUB):
                        kj = k[r0 + jj:r0 + jj + 1]
                        bj = b[r0 + jj:r0 + jj + 1]
                        t = q_blk * kj * jnp.exp(jnp.minimum(b_blk - bj, 0.0))
                        cj = jnp.sum(t, axis=-1, keepdims=True)
                        a_blk = jnp.where((sub_col == r0 + jj) & (sub_row >= jj), cj, a_blk)
                    a_scr[h, c * C + r0:c * C + r0 + GLA_SUB, c * C:(c + 1) * C] = a_blk

    for h, ks, q, k in heads():
        vs = slice(h * GLA_DV, (h + 1) * GLA_DV)
        v = v_ref[:, vs]
        b = b_scr[:, ks]
        q_in = (q * jnp.exp(b)).astype(BF16)
        st = st_scr[h]
        o_inter = []
        for c in range(n_chunks):
            cr = slice(c * C, (c + 1) * C)
            o_inter.append(lax.dot_general(q_in[cr], st.astype(BF16), (((1,), (1,)), ((), ())),
                                           preferred_element_type=F32))
            b_last = b[(c + 1) * C - 1:(c + 1) * C]
            k_out = (k[cr] * jnp.exp(b_last - b[cr])).astype(BF16)
            upd = lax.dot_general(v[cr], k_out, (((0,), (0,)), ((), ())),
                                  preferred_element_type=F32)
            st = st * jnp.exp(b_last) + upd
        st_scr[h] = st
        o = (jnp.concatenate(o_inter, axis=0)
             + jnp.dot(a_scr[h].astype(BF16), v, preferred_element_type=F32))

        gate = gate_ref[:, vs].astype(F32)
        y = _rms(o, gn) * (gate * jax.nn.sigmoid(gate))
        o_ref[:, vs] = y.astype(BF16)


def _gla(proj, ga, wup, b_alpha, gla_norm):
    S = proj.shape[0]
    R = GLA_ROWS
    return pl.pallas_call(
        _gla_kernel,
        grid=(S // R,),
        in_specs=[
            pl.BlockSpec((R, GLA_QK), lambda i: (i, COL_GQ // GLA_QK)),
            pl.BlockSpec((R, GLA_QK), lambda i: (i, COL_GK // GLA_QK)),
            pl.BlockSpec((R, GLA_V), lambda i: (i, COL_GV // GLA_V)),
            pl.BlockSpec((R, GLA_V), lambda i: (i, COL_GG // GLA_V)),
            pl.BlockSpec((R, V7X_LANES), lambda i: (i, 0)),
            pl.BlockSpec((V7X_LANES, GLA_QK), lambda i: (0, 0)),
            pl.BlockSpec((1, GLA_QK), lambda i: (0, 0)),
            pl.BlockSpec((1, GLA_DV), lambda i: (0, 0)),
        ],
        out_specs=pl.BlockSpec((R, GLA_V), lambda i: (i, 0)),
        out_shape=jax.ShapeDtypeStruct((S, GLA_V), BF16),
        scratch_shapes=[
            pltpu.VMEM((GLA_HEADS, GLA_DV, GLA_DK), F32),
            pltpu.VMEM((R, GLA_QK), F32),
            pltpu.VMEM((GLA_HEADS, R, R), F32),
        ],
        compiler_params=_params(("arbitrary",)),
        name="gla",
    )(proj, proj, proj, proj, ga, wup, b_alpha, gla_norm)


def _attn_kernel(slopes_ref, lq1_ref, lk1_ref, lq2_ref, lk2_ref, gn_ref,
                 q_ref, k_ref, vt_ref, o_ref, feat_scr, kn_scr, qa_scr, s_scr, m_scr, l_scr,
                 acc_scr):
    TQ, TK, QW, D = q_ref.shape[0], ATT_T, ATT_QW, DIFF_DQK
    h = pl.program_id(0)
    i = pl.program_id(1)
    n_kb = TQ // TK
    lane = lax.broadcasted_iota(jnp.int32, (TK, V7X_LANES), 1)

    slope2 = slopes_ref[h] * LOG2E
    blk_lane = lax.broadcasted_iota(jnp.int32, (1, V7X_LANES), 1)

    @pl.when(i == 0)
    def _():
        row = lax.broadcasted_iota(jnp.int32, (TK, V7X_LANES), 0)

        def fill(blk, carry):
            r0 = pl.multiple_of(blk * TK, TK)
            b = slope2 * (row + r0).astype(F32)
            hi = b.astype(BF16).astype(F32)
            r1 = b - hi
            mid = r1.astype(BF16).astype(F32)
            lo = r1 - mid
            feat = jnp.where(lane == 0, hi,
                             jnp.where(lane == 1, mid, jnp.where(lane == 2, lo, 0.0)))
            feat_scr[pl.ds(r0, TK), :] = feat.astype(BF16)
            new = []
            for c in range(2):
                run, vec = carry[2 * c], carry[2 * c + 1]
                kc = k_ref[pl.ds(r0, TK), c * D:(c + 1) * D].astype(F32)
                sq = jnp.sum(kc * kc, axis=-1, keepdims=True)
                run = jnp.maximum(run, jnp.max(sq, axis=0, keepdims=True))
                new += [run, jnp.where(blk_lane == blk, jnp.sqrt(run), vec)]
            return tuple(new)

        zero = jnp.zeros((1, 1), F32)
        zvec = jnp.zeros((1, V7X_LANES), F32)
        done = lax.fori_loop(0, feat_scr.shape[0] // TK, fill, (zero, zvec, zero, zvec))
        kn_scr[0] = done[1]
        kn_scr[1] = done[3]

    m_scr[...] = jnp.full_like(m_scr, NEG_BIG)
    l_scr[...] = jnp.zeros_like(l_scr)
    acc_scr[...] = jnp.zeros_like(acc_scr)

    q_lane = lax.broadcasted_iota(jnp.int32, (TQ, V7X_LANES), 1)
    q_ones = jnp.where(q_lane < 3, 1.0, 0.0).astype(BF16)
    for c in range(2):
        qa_scr[c, :, 0:D] = q_ref[:, c * D:(c + 1) * D]
        qa_scr[c, :, D:2 * D] = q_ones
    key_row = lax.broadcasted_iota(jnp.int32, (TK, QW), 0)
    query_col = lax.broadcasted_iota(jnp.int32, (TK, QW), 1)

    def slabs(key_off):
        return [sl for sl in range(TQ // QW)
                if key_off is None or key_off <= sl * QW + QW - 1]

    def scores(j, slot, key_off):
        r0 = pl.multiple_of(j * TK, TK)
        feat = feat_scr[pl.ds(r0, TK), :]
        for c in range(2):
            k = jnp.concatenate([k_ref[pl.ds(r0, TK), c * D:(c + 1) * D], feat], axis=1)
            for sl in slabs(key_off):
                qs = slice(sl * QW, (sl + 1) * QW)
                s_scr[slot, c, :, qs] = lax.dot_general(
                    k, qa_scr[c, qs, :], (((1,), (1,)), ((), ())),
                    preferred_element_type=F32)

    def accumulate(j, slot, key_off):
        vt = vt_ref[j]
        for c in range(2):
            for sl in slabs(key_off):
                q_lo = sl * QW
                qs = slice(q_lo, q_lo + QW)
                s = s_scr[slot, c, :, qs]
                if key_off is not None and key_off + TK - 1 > q_lo:
                    s = jnp.where(key_row + key_off <= query_col + q_lo, s, NEG_BIG)
                m_prev = m_scr[c, :, qs]
                m_new = jnp.maximum(m_prev, jnp.max(s, axis=0, keepdims=True))
                alpha = jnp.exp2(m_prev - m_new)
                p = jnp.exp2(s - m_new)
                l_scr[c, :, qs] = alpha * l_scr[c, :, qs] + jnp.sum(p, axis=0, keepdims=True)
                acc_scr[c, :, qs] = alpha * acc_scr[c, :, qs] + jnp.dot(
                    vt, p.astype(BF16), preferred_element_type=F32)
                m_scr[c, :, qs] = m_new

    assert n_kb == 2
    scores(2 * i, 0, 0)
    scores(2 * i + 1, 1, TK)
    accumulate(2 * i + 1, 1, TK)
    scores(jnp.maximum(2 * i - 1, 0), 1, None)
    accumulate(2 * i, 0, 0)

    last_pos = (blk_lane * TK + (TK - 1)).astype(F32)
    skip = blk_lane < 2 * i
    for c in range(2):
        qf = q_ref[:, c * D:(c + 1) * D].astype(F32)
        qn = jnp.sqrt(jnp.max(jnp.sum(qf * qf, axis=-1, keepdims=True), axis=0, keepdims=True))
        bound = qn * kn_scr[c] * ATT_NORM_SLACK + slope2 * last_pos
        m_min = jnp.min(m_scr[c], axis=-1, keepdims=True)
        skip = skip & (bound < m_min - ATT_SKIP_LOG2)
    first_kept = jnp.min(jnp.where(skip, float(V7X_LANES), blk_lane.astype(F32)))
    n_skip = jnp.minimum(first_kept.astype(jnp.int32), 2 * i)
    n_pairs = i - n_skip // 2

    def body(t, carry):
        lo = 2 * (i - 1 - t)
        scores(lo, 0, None)
        accumulate(lo + 1, 1, None)
        scores(jnp.maximum(lo - 1, 0), 1, None)
        accumulate(lo, 0, None)
        return carry

    lax.fori_loop(0, n_pairs, body, 0)

    lam = (jnp.exp(jnp.sum(lq1_ref[...] * lk1_ref[...], axis=-1, keepdims=True))
           - jnp.exp(jnp.sum(lq2_ref[...] * lk2_ref[...], axis=-1, keepdims=True))
           + LAMBDA_INIT)
    o1 = acc_scr[0] / l_scr[0]
    o2 = acc_scr[1] / l_scr[1]
    o = (o1 - lam * o2).T
    o_ref[...] = (_rms(o, gn_ref[...]) * (1.0 - LAMBDA_INIT)).astype(BF16)


def _diff_attn(proj, vt, slopes, lq1, lk1, lq2, lk2, diff_norm):
    S = proj.shape[0]
    TQ = min(ATT_TQ, S)
    W = 2 * DIFF_DQK
    vec = pl.BlockSpec((1, DIFF_DQK), lambda h, i: (0, 0))
    return pl.pallas_call(
        _attn_kernel,
        grid=(DIFF_HEADS, S // TQ),
        in_specs=[
            pl.BlockSpec(memory_space=pltpu.SMEM),
            vec, vec, vec, vec,
            pl.BlockSpec((1, DIFF_DV), lambda h, i: (0, 0)),
            pl.BlockSpec((TQ, W), lambda h, i: (i, COL_DQ // W + h)),
            pl.BlockSpec((S, W), lambda h, i: (0, COL_DK // W + h)),
            pl.BlockSpec((S // ATT_T, DIFF_DV, ATT_T), lambda h, i: (0, h, 0)),
        ],
        out_specs=pl.BlockSpec((TQ, DIFF_DV), lambda h, i: (i, h)),
        out_shape=jax.ShapeDtypeStruct((S, DIFF_V), BF16),
        scratch_shapes=[
            pltpu.VMEM((S, V7X_LANES), BF16),
            pltpu.VMEM((2, 1, V7X_LANES), F32),
            pltpu.VMEM((2, TQ, W), BF16),
            pltpu.VMEM((2, 2, ATT_T, TQ), F32),
            pltpu.VMEM((2, 1, TQ), F32),
            pltpu.VMEM((2, 1, TQ), F32),
            pltpu.VMEM((2, DIFF_DV, TQ), F32),
        ],
        compiler_params=_params(("arbitrary", "arbitrary")),
        name="diff_attn",
    )(slopes, lq1, lk1, lq2, lk2, diff_norm, proj, proj, vt)


def _out_proj_kernel(oa_ref, ob_ref, wo_ref, x_ref, gpost_ref, gpre_ref,
                     x1_ref, h2_ref):
    for r0 in range(0, x_ref.shape[0], OUT_RH):
        rs = slice(r0, r0 + OUT_RH)
        m = (jnp.dot(oa_ref[rs, :], wo_ref[0:GLA_V, :], preferred_element_type=F32)
             + jnp.dot(ob_ref[rs, :], wo_ref[GLA_V:GLA_V + DIFF_V, :],
                       preferred_element_type=F32))
        x1 = x_ref[rs, :] + _rms(m, gpost_ref[...])
        x1_ref[rs, :] = x1
        h2_ref[rs, :] = _rms(x1, gpre_ref[...]).astype(BF16)


def _out_proj(o_a, o_b, w_o, x, g_post, g_pre):
    S = x.shape[0]
    tm = min(512, S)
    row = lambda i: (i, 0)
    fixed = lambda i: (0, 0)
    return pl.pallas_call(
        _out_proj_kernel,
        grid=(S // tm,),
        in_specs=[
            pl.BlockSpec((tm, GLA_V), row),
            pl.BlockSpec((tm, DIFF_V), row),
            pl.BlockSpec((GLA_V + DIFF_V, D_MODEL), fixed),
            pl.BlockSpec((tm, D_MODEL), row),
            pl.BlockSpec((1, D_MODEL), fixed),
            pl.BlockSpec((1, D_MODEL), fixed),
        ],
        out_specs=[
            pl.BlockSpec((tm, D_MODEL), row),
            pl.BlockSpec((tm, D_MODEL), row),
        ],
        out_shape=[
            jax.ShapeDtypeStruct((S, D_MODEL), F32),
            jax.ShapeDtypeStruct((S, D_MODEL), BF16),
        ],
        compiler_params=_params(("parallel",)),
        name="out_proj",
    )(o_a, o_b, w_o, x, g_post, g_pre)


def _gelu_tanh(x):
    k = -2.0 * math.sqrt(2.0 / math.pi) * LOG2E
    return x / (1.0 + jnp.exp2(x * (k + (k * 0.044715) * (x * x))))


def _ffn_kernel(h2_ref, wa_ref, wb_ref, cw_ref, cb_ref, wout_ref, x1_ref, gpost_ref,
                o_ref, acc_scr, abuf_scr, halo_scr, *, tm):
    i = pl.program_id(0)
    j = pl.program_id(1)
    nj = pl.num_programs(1)
    P = V7X_SUBLANES

    @pl.when((i == 0) & (j == 0))
    def _():
        acc_scr[...] = jnp.zeros_like(acc_scr)
        halo_scr[...] = jnp.zeros_like(halo_scr)

    abuf_scr[0:P, :] = halo_scr[j]
    for r0 in range(0, tm, FFN_RH):
        rs = slice(r0, r0 + FFN_RH)
        h2 = h2_ref[rs, :]
        contrib = None
        for c0 in range(0, wa_ref.shape[1], FFN_CW):
            cs = slice(c0, c0 + FFN_CW)
            a = jnp.dot(h2, wa_ref[:, cs], preferred_element_type=F32)
            b = jnp.dot(h2, wb_ref[:, cs], preferred_element_type=F32)
            abuf_scr[P + r0:P + r0 + FFN_RH, cs] = a
            a1 = abuf_scr[P - 1 + r0:P - 1 + r0 + FFN_RH, cs]
            a2 = abuf_scr[P - 2 + r0:P - 2 + r0 + FFN_RH, cs]
            conv = (cw_ref[0:1, cs] * a2 + cw_ref[1:2, cs] * a1 + cw_ref[2:3, cs] * a
                    + cb_ref[:, cs])
            g = (_gelu_tanh(conv) * b).astype(BF16)
            part = jnp.dot(g, wout_ref[cs, :], preferred_element_type=F32)
            contrib = part if contrib is None else contrib + part
        acc_scr[rs, :] += contrib
    halo_scr[j] = abuf_scr[tm:tm + P, :]

    @pl.when(j == nj - 1)
    def _():
        o_ref[...] = x1_ref[...] + _rms(acc_scr[...], gpost_ref[...])
        acc_scr[...] = jnp.zeros_like(acc_scr)


def _ffn(h2, w_ffn_in, conv_w, conv_b, w_ffn_out, x1, g_post):
    S = h2.shape[0]
    tm = min(512, S)
    tn = 512
    nj = D_FF // tn
    row = lambda i, j: (i, 0)
    return pl.pallas_call(
        functools.partial(_ffn_kernel, tm=tm),
        grid=(S // tm, nj),
        in_specs=[
            pl.BlockSpec((tm, D_MODEL), row),
            pl.BlockSpec((D_MODEL, tn), lambda i, j: (0, j)),
            pl.BlockSpec((D_MODEL, tn), lambda i, j: (0, nj + j)),
            pl.BlockSpec((CONV_W, tn), lambda i, j: (0, j)),
            pl.BlockSpec((1, tn), lambda i, j: (0, j)),
            pl.BlockSpec((tn, D_MODEL), lambda i, j: (j, 0)),
            pl.BlockSpec((tm, D_MODEL), row),
            pl.BlockSpec((1, D_MODEL), lambda i, j: (0, 0)),
        ],
        out_specs=pl.BlockSpec((tm, D_MODEL), row),
        out_shape=jax.ShapeDtypeStruct((S, D_MODEL), F32),
        scratch_shapes=[
            pltpu.VMEM((tm, D_MODEL), F32),
            pltpu.VMEM((tm + V7X_SUBLANES, tn), F32),
            pltpu.VMEM((nj, V7X_SUBLANES, tn), F32),
        ],
        compiler_params=_params(("arbitrary", "arbitrary")),
        name="ffn",
    )(h2, w_ffn_in, w_ffn_in, conv_w, conv_b, w_ffn_out, x1, g_post)


def _layer(x, attn_pre_norm, w_in, w_alpha_up, b_alpha, gla_norm, lambda_q1, lambda_k1,
           lambda_q2, lambda_k2, diff_norm, w_o, attn_post_norm, ffn_pre_norm, w_ffn_in,
           conv_w, conv_b, w_ffn_out, ffn_post_norm):
    vec = lambda p: p.reshape(1, -1).astype(F32)

    w_in16 = w_in.astype(BF16)
    w_main = jnp.concatenate(
        [w_in16[:, :GA_OFFSET], w_in16[:, GA_OFFSET + GLA_RANK:]], axis=1)
    w_ga = jnp.pad(w_in16[:, GA_OFFSET:GA_OFFSET + GLA_RANK],
                   ((0, 0), (0, V7X_LANES - GLA_RANK)))
    colscale = jnp.concatenate([
        jnp.full((GLA_QK,), GLA_DK ** -0.5, F32),
        jnp.ones((COL_DQ - COL_GK,), F32),
        jnp.full((DIFF_QK,), DIFF_DQK ** -0.5 * LOG2E, F32),
        jnp.ones((PROJ_COLS - COL_DK,), F32),
    ]).reshape(1, PROJ_COLS)
    wup = jnp.pad(w_alpha_up, ((0, V7X_LANES - GLA_RANK), (0, 0))).astype(BF16)
    slopes = jnp.asarray(
        [2.0 ** (-8.0 * (h + 1) / DIFF_HEADS) for h in range(DIFF_HEADS)], F32)

    proj, ga, vt = _in_proj(x, vec(attn_pre_norm), w_main, w_ga, colscale)
    o_a = _gla(proj, ga, wup, vec(b_alpha), vec(gla_norm))
    o_b = _diff_attn(proj, vt, slopes, vec(lambda_q1), vec(lambda_k1), vec(lambda_q2),
                     vec(lambda_k2), vec(diff_norm))
    x1, h2 = _out_proj(o_a, o_b, w_o.astype(BF16), x, vec(attn_post_norm),
                       vec(ffn_pre_norm))
    return _ffn(h2, w_ffn_in.astype(BF16), conv_w.astype(F32), vec(conv_b),
                w_ffn_out.astype(BF16), x1, vec(ffn_post_norm))


def kernel(x, attn_pre_norm, w_in, w_alpha_up, b_alpha, gla_norm, lambda_q1, lambda_k1,
           lambda_q2, lambda_k2, diff_norm, w_o, attn_post_norm, ffn_pre_norm, w_ffn_in,
           conv_w, conv_b, w_ffn_out, ffn_post_norm):
    B = x.shape[0]
    depth = w_in.shape[0]
    assert depth == 1, "lambda_init is baked for a single layer"
    outs = []
    for bi in range(B):
        xb = x[bi]
        for l in range(depth):
            xb = _layer(xb, attn_pre_norm[l], w_in[l], w_alpha_up[l], b_alpha[l], gla_norm[l],
                        lambda_q1[l], lambda_k1[l], lambda_q2[l], lambda_k2[l], diff_norm[l],
                        w_o[l], attn_post_norm[l], ffn_pre_norm[l], w_ffn_in[l], conv_w[l],
                        conv_b[l], w_ffn_out[l], ffn_post_norm[l])
        outs.append(xb)
    return outs[0][None] if B == 1 else jnp.stack(outs, axis=0)
```

```python
import functools
import math

import jax
import jax.numpy as jnp
from jax import lax
from jax.experimental import pallas as pl
from jax.experimental.pallas import tpu as pltpu

F32 = jnp.float32
BF16 = jnp.bfloat16

D_MODEL = 2048
GLA_HEADS = 4
GLA_DK = 128
GLA_DV = 256
GLA_RANK = 16
GLA_TAU = 16.0
DIFF_HEADS = 4
DIFF_DQK = 128
DIFF_DV = 256
D_FF = 5632
CONV_W = 3
EPS = 1e-6
LAMBDA_INIT = 0.8 - 0.6 * math.exp(-0.3 * 0)
LOG2E = math.log2(math.e)

GLA_QK = GLA_HEADS * GLA_DK
GLA_V = GLA_HEADS * GLA_DV
DIFF_QK = DIFF_HEADS * 2 * DIFF_DQK
DIFF_V = DIFF_HEADS * DIFF_DV

COL_GQ = 0
COL_GK = COL_GQ + GLA_QK
COL_GV = COL_GK + GLA_QK
COL_GG = COL_GV + GLA_V
COL_DQ = COL_GG + GLA_V
COL_DK = COL_DQ + DIFF_QK
COL_DV = COL_DK + DIFF_QK
PROJ_COLS = COL_DV + DIFF_V
GA_OFFSET = GLA_QK + GLA_QK + GLA_V + GLA_V

V7X_LANES = 128
V7X_SUBLANES = 8
V7X_VMEM_LIMIT_BYTES = 56 * 1024 * 1024

IN_TN = 1536
OUT_RH = 256
FFN_TM = 1024
FFN_CW = 512
FFN_RH = 512
ATT_T = 512
ATT_TQ = 1024
ATT_QW = 256
ATT_SKIP_LOG2 = 160.0
ATT_NORM_SLACK = 1.001
GLA_CHUNK = 64
GLA_ROWS = 256
GLA_SUB = 16
GLA_SAFE_SPAN = 60.0
NEG_BIG = -1e30


def _rms(v, g):
    return v * lax.rsqrt(jnp.mean(v * v, axis=-1, keepdims=True) + EPS) * g


def _params(dims):
    return pltpu.CompilerParams(dimension_semantics=dims,
                                vmem_limit_bytes=V7X_VMEM_LIMIT_BYTES)


def _in_proj_kernel(x_ref, g_ref, w_ref, wga_ref, cs_ref, o_ref, ga_ref, vt_ref, h_scr,
                    *, tn):
    j = pl.program_id(1)

    @pl.when(j == 0)
    def _():
        h = _rms(x_ref[...], g_ref[...]).astype(BF16)
        h_scr[...] = h
        ga_ref[...] = jnp.dot(h, wga_ref[...], preferred_element_type=F32)

    acc = jnp.dot(h_scr[...], w_ref[...], preferred_element_type=F32)
    o_ref[...] = (acc * cs_ref[...]).astype(BF16)

    @pl.when(j == COL_DV // tn)
    def _():
        v0 = COL_DV % tn
        acc_t = acc[:, v0:v0 + DIFF_V].T.astype(BF16)
        for kb in range(vt_ref.shape[0]):
            vt_ref[kb] = acc_t[:, kb * ATT_T:(kb + 1) * ATT_T]


def _in_proj(x, g, w_main, w_ga, colscale):
    S = x.shape[0]
    tm = min(1024, S)
    tn = IN_TN
    return pl.pallas_call(
        functools.partial(_in_proj_kernel, tn=tn),
        grid=(S // tm, PROJ_COLS // tn),
        in_specs=[
            pl.BlockSpec((tm, D_MODEL), lambda i, j: (i, 0)),
            pl.BlockSpec((1, D_MODEL), lambda i, j: (0, 0)),
            pl.BlockSpec((D_MODEL, tn), lambda i, j: (0, j)),
            pl.BlockSpec((D_MODEL, V7X_LANES), lambda i, j: (0, 0)),
            pl.BlockSpec((1, tn), lambda i, j: (0, j)),
        ],
        out_specs=[
            pl.BlockSpec((tm, tn), lambda i, j: (i, j)),
            pl.BlockSpec((tm, V7X_LANES), lambda i, j: (i, 0)),
            pl.BlockSpec((tm // ATT_T, DIFF_V, ATT_T), lambda i, j: (i, 0, 0)),
        ],
        out_shape=[
            jax.ShapeDtypeStruct((S, PROJ_COLS), BF16),
            jax.ShapeDtypeStruct((S, V7X_LANES), F32),
            jax.ShapeDtypeStruct((S // ATT_T, DIFF_V, ATT_T), BF16),
        ],
        scratch_shapes=[pltpu.VMEM((tm, D_MODEL), BF16)],
        compiler_params=_params(("parallel", "arbitrary")),
        name="in_proj",
    )(x, g, w_main, w_ga, colscale)


def _log_sigmoid(x):
    return jnp.minimum(x, 0.0) - jnp.log(1.0 + jnp.exp(-jnp.abs(x)))


def _gla_kernel(q_ref, k_ref, v_ref, gate_ref, ga_ref, wup_ref, ba_ref, gn_ref,
                o_ref, st_scr, b_scr, a_scr):
    C, R = GLA_CHUNK, GLA_ROWS
    n_chunks = R // C

    @pl.when(pl.program_id(0) == 0)
    def _():
        st_scr[...] = jnp.zeros_like(st_scr)

    row = lax.broadcasted_iota(jnp.int32, (R, R), 0)
    col = lax.broadcasted_iota(jnp.int32, (R, R), 1)
    shift = C.bit_length() - 1
    in_chunk = (col <= row) & ((row >> shift) == (col >> shift))
    tril = jnp.where(in_chunk, 1.0, 0.0).astype(BF16)
    gn = gn_ref[...]

    z = jnp.dot(ga_ref[...].astype(BF16), wup_ref[...], preferred_element_type=F32) + ba_ref[...]
    la = _log_sigmoid(z) * (1.0 / GLA_TAU)
    la_hi = la.astype(BF16)
    la_lo = (la - la_hi.astype(F32)).astype(BF16)
    b_all = (jnp.dot(tril, la_hi, preferred_element_type=F32)
             + jnp.dot(tril, la_lo, preferred_element_type=F32))
    b_scr[...] = b_all
    b_first = jnp.concatenate(
        [jnp.broadcast_to(b_all[c * C:c * C + 1], (C, GLA_QK)) for c in range(n_chunks)], axis=0)
    span = jnp.max(b_first - b_all)
    small_span = span <= GLA_SAFE_SPAN

    def heads():
        for h in range(GLA_HEADS):
            ks = slice(h * GLA_DK, (h + 1) * GLA_DK)
            yield h, ks, q_ref[:, ks].astype(F32), k_ref[:, ks].astype(F32)

    @pl.when(small_span)
    def _():
        for h, ks, q, k in heads():
            d = b_first[:, ks] - b_scr[:, ks]
            q_t = (q * jnp.exp(-d)).astype(BF16)
            k_t = (k * jnp.exp(d)).astype(BF16)
            s = lax.dot_general(q_t, k_t, (((1,), (1,)), ((), ())),
                                preferred_element_type=F32)
            a_scr[h] = jnp.where(in_chunk, s, 0.0)

    @pl.when(jnp.logical_not(small_span))
    def _():
        a_scr[...] = jnp.zeros_like(a_scr)
        sub_row = lax.broadcasted_iota(jnp.int32, (GLA_SUB, C), 0)
        sub_col = lax.broadcasted_iota(jnp.int32, (GLA_SUB, C), 1)
        for h, ks, q_all, k_all in heads():
            for c in range(n_chunks):
                q = q_all[c * C:(c + 1) * C]
                k = k_all[c * C:(c + 1) * C]
                b = b_scr[c * C:(c + 1) * C, ks]
                for blk in range(C // GLA_SUB):
                    r0 = blk * GLA_SUB
                    b_blk = b[r0:r0 + GLA_SUB]
                    q_blk = q[r0:r0 + GLA_SUB]
                    b_ref0 = b[r0:r0 + 1]
                    a_blk = jnp.zeros((GLA_SUB, C), F32)
                    if blk > 0:
                        q_t = (q_blk * jnp.exp(b_blk - b_ref0)).astype(BF16)
                        k_t = (k * jnp.exp(jnp.minimum(b_ref0 - b, 0.0))).astype(BF16)
                        off = lax.dot_general(q_t, k_t, (((1,), (1,)), ((), ())),
                                              preferred_element_type=F32)
                        a_blk = jnp.where(sub_col < r0, off, 0.0)
                    for jj in range(GLA_SUB):
                        kj = k[r0 + jj:r0 + jj + 1]
                        bj = b[r0 + jj:r0 + jj + 1]
                        t = q_blk * kj * jnp.exp(jnp.minimum(b_blk - bj, 0.0))
                        cj = jnp.sum(t, axis=-1, keepdims=True)
                        a_blk = jnp.where((sub_col == r0 + jj) & (sub_row >= jj), cj, a_blk)
                    a_scr[h, c * C + r0:c * C + r0 + GLA_SUB, c * C:(c + 1) * C] = a_blk

    for h, ks, q, k in heads():
        vs = slice(h * GLA_DV, (h + 1) * GLA_DV)
        v = v_ref[:, vs]
        b = b_scr[:, ks]
        q_in = (q * jnp.exp(b)).astype(BF16)
        st = st_scr[h]
        o_inter = []
        for c in range(n_chunks):
            cr = slice(c * C, (c + 1) * C)
            o_inter.append(lax.dot_general(q_in[cr], st.astype(BF16), (((1,), (1,)), ((), ())),
                                           preferred_element_type=F32))
            b_last = b[(c + 1) * C - 1:(c + 1) * C]
            k_out = (k[cr] * jnp.exp(b_last - b[cr])).astype(BF16)
            upd = lax.dot_general(v[cr], k_out, (((0,), (0,)), ((), ())),
                                  preferred_element_type=F32)
            st = st * jnp.exp(b_last) + upd
        st_scr[h] = st
        o = (jnp.concatenate(o_inter, axis=0)
             + jnp.dot(a_scr[h].astype(BF16), v, preferred_element_type=F32))

        gate = gate_ref[:, vs].astype(F32)
        y = _rms(o, gn) * (gate * jax.nn.sigmoid(gate))
        o_ref[:, vs] = y.astype(BF16)


def _gla(proj, ga, wup, b_alpha, gla_norm):
    S = proj.shape[0]
    R = GLA_ROWS
    return pl.pallas_call(
        _gla_kernel,
        grid=(S // R,),
        in_specs=[
            pl.BlockSpec((R, GLA_QK), lambda i: (i, COL_GQ // GLA_QK)),
            pl.BlockSpec((R, GLA_QK), lambda i: (i, COL_GK // GLA_QK)),
            pl.BlockSpec((R, GLA_V), lambda i: (i, COL_GV // GLA_V)),
            pl.BlockSpec((R, GLA_V), lambda i: (i, COL_GG // GLA_V)),
            pl.BlockSpec((R, V7X_LANES), lambda i: (i, 0)),
            pl.BlockSpec((V7X_LANES, GLA_QK), lambda i: (0, 0)),
            pl.BlockSpec((1, GLA_QK), lambda i: (0, 0)),
            pl.BlockSpec((1, GLA_DV), lambda i: (0, 0)),
        ],
        out_specs=pl.BlockSpec((R, GLA_V), lambda i: (i, 0)),
        out_shape=jax.ShapeDtypeStruct((S, GLA_V), BF16),
        scratch_shapes=[
            pltpu.VMEM((GLA_HEADS, GLA_DV, GLA_DK), F32),
            pltpu.VMEM((R, GLA_QK), F32),
            pltpu.VMEM((GLA_HEADS, R, R), F32),
        ],
        compiler_params=_params(("arbitrary",)),
        name="gla",
    )(proj, proj, proj, proj, ga, wup, b_alpha, gla_norm)


def _attn_kernel(slopes_ref, lq1_ref, lk1_ref, lq2_ref, lk2_ref, gn_ref,
                 q_ref, k_ref, vt_ref, o_ref, feat_scr, kn_scr, qa_scr, s_scr, m_scr, l_scr,
                 acc_scr):
    TQ, TK, QW, D = q_ref.shape[0], ATT_T, ATT_QW, DIFF_DQK
    h = pl.program_id(0)
    i = pl.program_id(1)
    n_kb = TQ // TK
    lane = lax.broadcasted_iota(jnp.int32, (TK, V7X_LANES), 1)

    slope2 = slopes_ref[h] * LOG2E
    blk_lane = lax.broadcasted_iota(jnp.int32, (1, V7X_LANES), 1)

    @pl.when(i == 0)
    def _():
        row = lax.broadcasted_iota(jnp.int32, (TK, V7X_LANES), 0)

        def fill(blk, carry):
            r0 = pl.multiple_of(blk * TK, TK)
            b = slope2 * (row + r0).astype(F32)
            hi = b.astype(BF16).astype(F32)
            r1 = b - hi
            mid = r1.astype(BF16).astype(F32)
            lo = r1 - mid
            feat = jnp.where(lane == 0, hi,
                             jnp.where(lane == 1, mid, jnp.where(lane == 2, lo, 0.0)))
            feat_scr[pl.ds(r0, TK), :] = feat.astype(BF16)
            new = []
            for c in range(2):
                run, vec = carry[2 * c], carry[2 * c + 1]
                kc = k_ref[pl.ds(r0, TK), c * D:(c + 1) * D].astype(F32)
                sq = jnp.sum(kc * kc, axis=-1, keepdims=True)
                run = jnp.maximum(run, jnp.max(sq, axis=0, keepdims=True))
                new += [run, jnp.where(blk_lane == blk, jnp.sqrt(run), vec)]
            return tuple(new)

        zero = jnp.zeros((1, 1), F32)
        zvec = jnp.zeros((1, V7X_LANES), F32)
        done = lax.fori_loop(0, feat_scr.shape[0] // TK, fill, (zero, zvec, zero, zvec))
        kn_scr[0] = done[1]
        kn_scr[1] = done[3]

    m_scr[...] = jnp.full_like(m_scr, NEG_BIG)
    l_scr[...] = jnp.zeros_like(l_scr)
    acc_scr[...] = jnp.zeros_like(acc_scr)

    q_lane = lax.broadcasted_iota(jnp.int32, (TQ, V7X_LANES), 1)
    q_ones = jnp.where(q_lane < 3, 1.0, 0.0).astype(BF16)
    for c in range(2):
        qa_scr[c, :, 0:D] = q_ref[:, c * D:(c + 1) * D]
        qa_scr[c, :, D:2 * D] = q_ones
    key_row = lax.broadcasted_iota(jnp.int32, (TK, QW), 0)
    query_col = lax.broadcasted_iota(jnp.int32, (TK, QW), 1)

    def slabs(key_off):
        return [sl for sl in range(TQ // QW)
                if key_off is None or key_off <= sl * QW + QW - 1]

    def scores(j, slot, key_off):
        r0 = pl.multiple_of(j * TK, TK)
        feat = feat_scr[pl.ds(r0, TK), :]
        for c in range(2):
            k = jnp.concatenate([k_ref[pl.ds(r0, TK), c * D:(c + 1) * D], feat], axis=1)
            for sl in slabs(key_off):
                qs = slice(sl * QW, (sl + 1) * QW)
                s_scr[slot, c, :, qs] = lax.dot_general(
                    k, qa_scr[c, qs, :], (((1,), (1,)), ((), ())),
                    preferred_element_type=F32)

    def accumulate(j, slot, key_off):
        vt = vt_ref[j]
        for c in range(2):
            for sl in slabs(key_off):
                q_lo = sl * QW
                qs = slice(q_lo, q_lo + QW)
                s = s_scr[slot, c, :, qs]
                if key_off is not None and key_off + TK - 1 > q_lo:
                    s = jnp.where(key_row + key_off <= query_col + q_lo, s, NEG_BIG)
                m_prev = m_scr[c, :, qs]
                m_new = jnp.maximum(m_prev, jnp.max(s, axis=0, keepdims=True))
                alpha = jnp.exp2(m_prev - m_new)
                p = jnp.exp2(s - m_new)
                l_scr[c, :, qs] = alpha * l_scr[c, :, qs] + jnp.sum(p, axis=0, keepdims=True)
                acc_scr[c, :, qs] = alpha * acc_scr[c, :, qs] + jnp.dot(
                    vt, p.astype(BF16), preferred_element_type=F32)
                m_scr[c, :, qs] = m_new

    assert n_kb == 2
    scores(2 * i, 0, 0)
    scores(2 * i + 1, 1, TK)
    accumulate(2 * i + 1, 1, TK)
    scores(jnp.maximum(2 * i - 1, 0), 1, None)
    accumulate(2 * i, 0, 0)

    last_pos = (blk_lane * TK + (TK - 1)).astype(F32)
    skip = blk_lane < 2 * i
    for c in range(2):
        qf = q_ref[:, c * D:(c + 1) * D].astype(F32)
        qn = jnp.sqrt(jnp.max(jnp.sum(qf * qf, axis=-1, keepdims=True), axis=0, keepdims=True))
        bound = qn * kn_scr[c] * ATT_NORM_SLACK + slope2 * last_pos
        m_min = jnp.min(m_scr[c], axis=-1, keepdims=True)
        skip = skip & (bound < m_min - ATT_SKIP_LOG2)
    first_kept = jnp.min(jnp.where(skip, float(V7X_LANES), blk_lane.astype(F32)))
    n_skip = jnp.minimum(first_kept.astype(jnp.int32), 2 * i)
    n_pairs = i - n_skip // 2

    def body(t, carry):
        lo = 2 * (i - 1 - t)
        scores(lo, 0, None)
        accumulate(lo + 1, 1, None)
        scores(jnp.maximum(lo - 1, 0), 1, None)
        accumulate(lo, 0, None)
        return carry

    lax.fori_loop(0, n_pairs, body, 0)

    lam = (jnp.exp(jnp.sum(lq1_ref[...] * lk1_ref[...], axis=-1, keepdims=True))
           - jnp.exp(jnp.sum(lq2_ref[...] * lk2_ref[...], axis=-1, keepdims=True))
           + LAMBDA_INIT)
    o1 = acc_scr[0] / l_scr[0]
    o2 = acc_scr[1] / l_scr[1]
    o = (o1 - lam * o2).T
    o_ref[...] = (_rms(o, gn_ref[...]) * (1.0 - LAMBDA_INIT)).astype(BF16)


def _diff_attn(proj, vt, slopes, lq1, lk1, lq2, lk2, diff_norm):
    S = proj.shape[0]
    TQ = min(ATT_TQ, S)
    W = 2 * DIFF_DQK
    vec = pl.BlockSpec((1, DIFF_DQK), lambda h, i: (0, 0))
    return pl.pallas_call(
        _attn_kernel,
        grid=(DIFF_HEADS, S // TQ),
        in_specs=[
            pl.BlockSpec(memory_space=pltpu.SMEM),
            vec, vec, vec, vec,
            pl.BlockSpec((1, DIFF_DV), lambda h, i: (0, 0)),
            pl.BlockSpec((TQ, W), lambda h, i: (i, COL_DQ // W + h)),
            pl.BlockSpec((S, W), lambda h, i: (0, COL_DK // W + h)),
            pl.BlockSpec((S // ATT_T, DIFF_DV, ATT_T), lambda h, i: (0, h, 0)),
        ],
        out_specs=pl.BlockSpec((TQ, DIFF_DV), lambda h, i: (i, h)),
        out_shape=jax.ShapeDtypeStruct((S, DIFF_V), BF16),
        scratch_shapes=[
            pltpu.VMEM((S, V7X_LANES), BF16),
            pltpu.VMEM((2, 1, V7X_LANES), F32),
            pltpu.VMEM((2, TQ, W), BF16),
            pltpu.VMEM((2, 2, ATT_T, TQ), F32),
            pltpu.VMEM((2, 1, TQ), F32),
            pltpu.VMEM((2, 1, TQ), F32),
            pltpu.VMEM((2, DIFF_DV, TQ), F32),
        ],
        compiler_params=_params(("arbitrary", "arbitrary")),
        name="diff_attn",
    )(slopes, lq1, lk1, lq2, lk2, diff_norm, proj, proj, vt)


def _out_proj_kernel(oa_ref, ob_ref, wo_ref, x_ref, gpost_ref, gpre_ref,
                     x1_ref, h2_ref):
    for r0 in range(0, x_ref.shape[0], OUT_RH):
        rs = slice(r0, r0 + OUT_RH)
        m = (jnp.dot(oa_ref[rs, :], wo_ref[0:GLA_V, :], preferred_element_type=F32)
             + jnp.dot(ob_ref[rs, :], wo_ref[GLA_V:GLA_V + DIFF_V, :],
                       preferred_element_type=F32))
        x1 = x_ref[rs, :] + _rms(m, gpost_ref[...])
        x1_ref[rs, :] = x1
        h2_ref[rs, :] = _rms(x1, gpre_ref[...]).astype(BF16)


def _out_proj(o_a, o_b, w_o, x, g_post, g_pre):
    S = x.shape[0]
    tm = min(512, S)
    row = lambda i: (i, 0)
    fixed = lambda i: (0, 0)
    return pl.pallas_call(
        _out_proj_kernel,
        grid=(S // tm,),
        in_specs=[
            pl.BlockSpec((tm, GLA_V), row),
            pl.BlockSpec((tm, DIFF_V), row),
            pl.BlockSpec((GLA_V + DIFF_V, D_MODEL), fixed),
            pl.BlockSpec((tm, D_MODEL), row),
            pl.BlockSpec((1, D_MODEL), fixed),
            pl.BlockSpec((1, D_MODEL), fixed),
        ],
        out_specs=[
            pl.BlockSpec((tm, D_MODEL), row),
            pl.BlockSpec((tm, D_MODEL), row),
        ],
        out_shape=[
            jax.ShapeDtypeStruct((S, D_MODEL), F32),
            jax.ShapeDtypeStruct((S, D_MODEL), BF16),
        ],
        compiler_params=_params(("parallel",)),
        name="out_proj",
    )(o_a, o_b, w_o, x, g_post, g_pre)


def _gelu_tanh(x):
    k = -2.0 * math.sqrt(2.0 / math.pi) * LOG2E
    return x / (1.0 + jnp.exp2(x * (k + (k * 0.044715) * (x * x))))


def _ffn_kernel(h2_ref, wa_ref, wb_ref, cw_ref, cb_ref, wout_ref, x1_ref, gpost_ref,
                o_ref, abuf_scr, halo_scr, *, tm):
    i = pl.program_id(0)
    j = pl.program_id(1)
    nj = pl.num_programs(1)
    P = V7X_SUBLANES

    @pl.when(j == 0)
    def _():
        o_ref[...] = jnp.zeros_like(o_ref)

    @pl.when((i == 0) & (j == 0))
    def _():
        halo_scr[...] = jnp.zeros_like(halo_scr)

    abuf_scr[0:P, :] = halo_scr[j]
    for r0 in range(0, tm, FFN_RH):
        rs = slice(r0, r0 + FFN_RH)
        h2 = h2_ref[rs, :]
        contrib = None
        for c0 in range(0, wa_ref.shape[1], FFN_CW):
            cs = slice(c0, c0 + FFN_CW)
            a = jnp.dot(h2, wa_ref[:, cs], preferred_element_type=F32)
            b = jnp.dot(h2, wb_ref[:, cs], preferred_element_type=F32)
            abuf_scr[P + r0:P + r0 + FFN_RH, cs] = a
            a1 = abuf_scr[P - 1 + r0:P - 1 + r0 + FFN_RH, cs]
            a2 = abuf_scr[P - 2 + r0:P - 2 + r0 + FFN_RH, cs]
            conv = (cw_ref[0:1, cs] * a2 + cw_ref[1:2, cs] * a1 + cw_ref[2:3, cs] * a
                    + cb_ref[:, cs])
            g = (_gelu_tanh(conv) * b).astype(BF16)
            part = jnp.dot(g, wout_ref[cs, :], preferred_element_type=F32)
            contrib = part if contrib is None else contrib + part
        o_ref[rs, :] += contrib
    halo_scr[j] = abuf_scr[tm:tm + P, :]

    @pl.when(j == nj - 1)
    def _():
        for r0 in range(0, tm, FFN_RH):
            rs = slice(r0, r0 + FFN_RH)
            o_ref[rs, :] = x1_ref[rs, :] + _rms(o_ref[rs, :], gpost_ref[...])


def _ffn(h2, w_ffn_in, conv_w, conv_b, w_ffn_out, x1, g_post):
    S = h2.shape[0]
    tm = min(FFN_TM, S)
    tn = 512
    nj = D_FF // tn
    row = lambda i, j: (i, 0)
    return pl.pallas_call(
        functools.partial(_ffn_kernel, tm=tm),
        grid=(S // tm, nj),
        in_specs=[
            pl.BlockSpec((tm, D_MODEL), row),
            pl.BlockSpec((D_MODEL, tn), lambda i, j: (0, j)),
            pl.BlockSpec((D_MODEL, tn), lambda i, j: (0, nj + j)),
            pl.BlockSpec((CONV_W, tn), lambda i, j: (0, j)),
            pl.BlockSpec((1, tn), lambda i, j: (0, j)),
            pl.BlockSpec((tn, D_MODEL), lambda i, j: (j, 0)),
            pl.BlockSpec((tm, D_MODEL), row, pipeline_mode=pl.Buffered(1)),
            pl.BlockSpec((1, D_MODEL), lambda i, j: (0, 0)),
        ],
        out_specs=pl.BlockSpec((tm, D_MODEL), row),
        out_shape=jax.ShapeDtypeStruct((S, D_MODEL), F32),
        scratch_shapes=[
            pltpu.VMEM((tm + V7X_SUBLANES, tn), F32),
            pltpu.VMEM((nj, V7X_SUBLANES, tn), F32),
        ],
        compiler_params=_params(("arbitrary", "arbitrary")),
        name="ffn",
    )(h2, w_ffn_in, w_ffn_in, conv_w, conv_b, w_ffn_out, x1, g_post)


def _layer(x, attn_pre_norm, w_in, w_alpha_up, b_alpha, gla_norm, lambda_q1, lambda_k1,
           lambda_q2, lambda_k2, diff_norm, w_o, attn_post_norm, ffn_pre_norm, w_ffn_in,
           conv_w, conv_b, w_ffn_out, ffn_post_norm):
    vec = lambda p: p.reshape(1, -1).astype(F32)

    w_in16 = w_in.astype(BF16)
    w_main = jnp.concatenate(
        [w_in16[:, :GA_OFFSET], w_in16[:, GA_OFFSET + GLA_RANK:]], axis=1)
    w_ga = jnp.pad(w_in16[:, GA_OFFSET:GA_OFFSET + GLA_RANK],
                   ((0, 0), (0, V7X_LANES - GLA_RANK)))
    colscale = jnp.concatenate([
        jnp.full((GLA_QK,), GLA_DK ** -0.5, F32),
        jnp.ones((COL_DQ - COL_GK,), F32),
        jnp.full((DIFF_QK,), DIFF_DQK ** -0.5 * LOG2E, F32),
        jnp.ones((PROJ_COLS - COL_DK,), F32),
    ]).reshape(1, PROJ_COLS)
    wup = jnp.pad(w_alpha_up, ((0, V7X_LANES - GLA_RANK), (0, 0))).astype(BF16)
    slopes = jnp.asarray(
        [2.0 ** (-8.0 * (h + 1) / DIFF_HEADS) for h in range(DIFF_HEADS)], F32)

    proj, ga, vt = _in_proj(x, vec(attn_pre_norm), w_main, w_ga, colscale)
    o_a = _gla(proj, ga, wup, vec(b_alpha), vec(gla_norm))
    o_b = _diff_attn(proj, vt, slopes, vec(lambda_q1), vec(lambda_k1), vec(lambda_q2),
                     vec(lambda_k2), vec(diff_norm))
    x1, h2 = _out_proj(o_a, o_b, w_o.astype(BF16), x, vec(attn_post_norm),
                       vec(ffn_pre_norm))
    return _ffn(h2, w_ffn_in.astype(BF16), conv_w.astype(F32), vec(conv_b),
                w_ffn_out.astype(BF16), x1, vec(ffn_post_norm))


def kernel(x, attn_pre_norm, w_in, w_alpha_up, b_alpha, gla_norm, lambda_q1, lambda_k1,
           lambda_q2, lambda_k2, diff_norm, w_o, attn_post_norm, ffn_pre_norm, w_ffn_in,
           conv_w, conv_b, w_ffn_out, ffn_post_norm):
    B = x.shape[0]
    depth = w_in.shape[0]
    assert depth == 1, "lambda_init is baked for a single layer"
    outs = []
    for bi in range(B):
        xb = x[bi]
        for l in range(depth):
            xb = _layer(xb, attn_pre_norm[l], w_in[l], w_alpha_up[l], b_alpha[l], gla_norm[l],
                        lambda_q1[l], lambda_k1[l], lambda_q2[l], lambda_k2[l], diff_norm[l],
                        w_o[l], attn_post_norm[l], ffn_pre_norm[l], w_ffn_in[l], conv_w[l],
                        conv_b[l], w_ffn_out[l], ffn_post_norm[l])
        outs.append(xb)
    return outs[0][None] if B == 1 else jnp.stack(outs, axis=0)
```

```python
import functools
import math

import jax
import jax.numpy as jnp
from jax import lax
from jax.experimental import pallas as pl
from jax.experimental.pallas import tpu as pltpu

F32 = jnp.float32
BF16 = jnp.bfloat16

D_MODEL = 2048
GLA_HEADS = 4
GLA_DK = 128
GLA_DV = 256
GLA_RANK = 16
GLA_TAU = 16.0
DIFF_HEADS = 4
DIFF_DQK = 128
DIFF_DV = 256
D_FF = 5632
CONV_W = 3
EPS = 1e-6
LAMBDA_INIT = 0.8 - 0.6 * math.exp(-0.3 * 0)
LOG2E = math.log2(math.e)

GLA_QK = GLA_HEADS * GLA_DK
GLA_V = GLA_HEADS * GLA_DV
DIFF_QK = DIFF_HEADS * 2 * DIFF_DQK
DIFF_V = DIFF_HEADS * DIFF_DV

COL_GQ = 0
COL_GK = COL_GQ + GLA_QK
COL_GV = COL_GK + GLA_QK
COL_GG = COL_GV + GLA_V
COL_DQ = COL_GG + GLA_V
COL_DK = COL_DQ + DIFF_QK
COL_DV = COL_DK + DIFF_QK
PROJ_COLS = COL_DV + DIFF_V
GA_OFFSET = GLA_QK + GLA_QK + GLA_V + GLA_V

V7X_LANES = 128
V7X_SUBLANES = 8
V7X_VMEM_LIMIT_BYTES = 56 * 1024 * 1024

IN_TN = 1536
OUT_RH = 256
FFN_CW = 512
FFN_RH = 512
ATT_T = 512
ATT_TQ = 1024
ATT_QW = 256
ATT_SKIP_LOG2 = 160.0
ATT_NORM_SLACK = 1.001
GLA_CHUNK = 64
GLA_ROWS = 256
GLA_SUB = 16
GLA_SAFE_SPAN = 60.0
NEG_BIG = -1e30


def _rms(v, g):
    return v * lax.rsqrt(jnp.mean(v * v, axis=-1, keepdims=True) + EPS) * g


def _params(dims):
    return pltpu.CompilerParams(dimension_semantics=dims,
                                vmem_limit_bytes=V7X_VMEM_LIMIT_BYTES)


def _in_proj_kernel(x_ref, g_ref, w_ref, wga_ref, cs_ref, o_ref, ga_ref, vt_ref, h_scr,
                    *, tn):
    j = pl.program_id(1)

    @pl.when(j == 0)
    def _():
        h = _rms(x_ref[...], g_ref[...]).astype(BF16)
        h_scr[...] = h
        ga_ref[...] = jnp.dot(h, wga_ref[...], preferred_element_type=F32)

    acc = jnp.dot(h_scr[...], w_ref[...], preferred_element_type=F32)
    o_ref[...] = (acc * cs_ref[...]).astype(BF16)

    @pl.when(j == COL_DV // tn)
    def _():
        v0 = COL_DV % tn
        acc_t = acc[:, v0:v0 + DIFF_V].T.astype(BF16)
        for kb in range(vt_ref.shape[0]):
            vt_ref[kb] = acc_t[:, kb * ATT_T:(kb + 1) * ATT_T]


def _in_proj(x, g, w_main, w_ga, colscale):
    S = x.shape[0]
    tm = min(1024, S)
    tn = IN_TN
    return pl.pallas_call(
        functools.partial(_in_proj_kernel, tn=tn),
        grid=(S // tm, PROJ_COLS // tn),
        in_specs=[
            pl.BlockSpec((tm, D_MODEL), lambda i, j: (i, 0)),
            pl.BlockSpec((1, D_MODEL), lambda i, j: (0, 0)),
            pl.BlockSpec((D_MODEL, tn), lambda i, j: (0, j)),
            pl.BlockSpec((D_MODEL, V7X_LANES), lambda i, j: (0, 0)),
            pl.BlockSpec((1, tn), lambda i, j: (0, j)),
        ],
        out_specs=[
            pl.BlockSpec((tm, tn), lambda i, j: (i, j)),
            pl.BlockSpec((tm, V7X_LANES), lambda i, j: (i, 0)),
            pl.BlockSpec((tm // ATT_T, DIFF_V, ATT_T), lambda i, j: (i, 0, 0)),
        ],
        out_shape=[
            jax.ShapeDtypeStruct((S, PROJ_COLS), BF16),
            jax.ShapeDtypeStruct((S, V7X_LANES), F32),
            jax.ShapeDtypeStruct((S // ATT_T, DIFF_V, ATT_T), BF16),
        ],
        scratch_shapes=[pltpu.VMEM((tm, D_MODEL), BF16)],
        compiler_params=_params(("parallel", "arbitrary")),
        name="in_proj",
    )(x, g, w_main, w_ga, colscale)


def _log_sigmoid(x):
    return jnp.minimum(x, 0.0) - jnp.log(1.0 + jnp.exp(-jnp.abs(x)))


def _gla_kernel(q_ref, k_ref, v_ref, gate_ref, ga_ref, wup_ref, ba_ref, gn_ref,
                o_ref, st_scr, b_scr, a_scr):
    C, R = GLA_CHUNK, GLA_ROWS
    n_chunks = R // C

    @pl.when(pl.program_id(0) == 0)
    def _():
        st_scr[...] = jnp.zeros_like(st_scr)

    row = lax.broadcasted_iota(jnp.int32, (R, R), 0)
    col = lax.broadcasted_iota(jnp.int32, (R, R), 1)
    shift = C.bit_length() - 1
    in_chunk = (col <= row) & ((row >> shift) == (col >> shift))
    tril = jnp.where(in_chunk, 1.0, 0.0).astype(BF16)
    gn = gn_ref[...]

    z = jnp.dot(ga_ref[...].astype(BF16), wup_ref[...], preferred_element_type=F32) + ba_ref[...]
    la = _log_sigmoid(z) * (1.0 / GLA_TAU)
    la_hi = la.astype(BF16)
    la_lo = (la - la_hi.astype(F32)).astype(BF16)
    b_all = (jnp.dot(tril, la_hi, preferred_element_type=F32)
             + jnp.dot(tril, la_lo, preferred_element_type=F32))
    b_scr[...] = b_all
    b_first = jnp.concatenate(
        [jnp.broadcast_to(b_all[c * C:c * C + 1], (C, GLA_QK)) for c in range(n_chunks)], axis=0)
    span = jnp.max(b_first - b_all)
    small_span = span <= GLA_SAFE_SPAN

    def heads():
        for h in range(GLA_HEADS):
            ks = slice(h * GLA_DK, (h + 1) * GLA_DK)
            yield h, ks, q_ref[:, ks].astype(F32), k_ref[:, ks].astype(F32)

    @pl.when(small_span)
    def _():
        for h, ks, q, k in heads():
            d = b_first[:, ks] - b_scr[:, ks]
            q_t = (q * jnp.exp(-d)).astype(BF16)
            k_t = (k * jnp.exp(d)).astype(BF16)
            s = lax.dot_general(q_t, k_t, (((1,), (1,)), ((), ())),
                                preferred_element_type=F32)
            a_scr[h] = jnp.where(in_chunk, s, 0.0)

    @pl.when(jnp.logical_not(small_span))
    def _():
        a_scr[...] = jnp.zeros_like(a_scr)
        sub_row = lax.broadcasted_iota(jnp.int32, (GLA_SUB, C), 0)
        sub_col = lax.broadcasted_iota(jnp.int32, (GLA_SUB, C), 1)
        for h, ks, q_all, k_all in heads():
            for c in range(n_chunks):
                q = q_all[c * C:(c + 1) * C]
                k = k_all[c * C:(c + 1) * C]
                b = b_scr[c * C:(c + 1) * C, ks]
                for blk in range(C // GLA_SUB):
                    r0 = blk * GLA_SUB
                    b_blk = b[r0:r0 + GLA_SUB]
                    q_blk = q[r0:r0 + GLA_SUB]
                    b_ref0 = b[r0:r0 + 1]
                    a_blk = jnp.zeros((GLA_SUB, C), F32)
                    if blk > 0:
                        q_t = (q_blk * jnp.exp(b_blk - b_ref0)).astype(BF16)
                        k_t = (k * jnp.exp(jnp.minimum(b_ref0 - b, 0.0))).astype(BF16)
                        off = lax.dot_general(q_t, k_t, (((1,), (1,)), ((), ())),
                                              preferred_element_type=F32)
                        a_blk = jnp.where(sub_col < r0, off, 0.0)
                    for jj in range(GLA_SUB):
                        kj = k[r0 + jj:r0 + jj + 1]
                        bj = b[r0 + jj:r0 + jj + 1]
                        t = q_blk * kj * jnp.exp(jnp.minimum(b_blk - bj, 0.0))
                        cj = jnp.sum(t, axis=-1, keepdims=True)
                        a_blk = jnp.where((sub_col == r0 + jj) & (sub_row >= jj), cj, a_blk)
                    a_scr[h, c * C + r0:c * C + r0 + GLA_SUB, c * C:(c + 1) * C] = a_blk

    for h, ks, q, k in heads():
        vs = slice(h * GLA_DV, (h + 1) * GLA_DV)
        v = v_ref[:, vs]
        b = b_scr[:, ks]
        q_in = (q * jnp.exp(b)).astype(BF16)
        st = st_scr[h]
        o_inter = []
        for c in range(n_chunks):
            cr = slice(c * C, (c + 1) * C)
            o_inter.append(lax.dot_general(q_in[cr], st.astype(BF16), (((1,), (1,)), ((), ())),
                                           preferred_element_type=F32))
            b_last = b[(c + 1) * C - 1:(c + 1) * C]
            k_out = (k[cr] * jnp.exp(b_last - b[cr])).astype(BF16)
            upd = lax.dot_general(v[cr], k_out, (((0,), (0,)), ((), ())),
                                  preferred_element_type=F32)
            st = st * jnp.exp(b_last) + upd
        st_scr[h] = st
        o = (jnp.concatenate(o_inter, axis=0)
             + jnp.dot(a_scr[h].astype(BF16), v, preferred_element_type=F32))

        gate = gate_ref[:, vs].astype(F32)
        y = _rms(o, gn) * (gate * jax.nn.sigmoid(gate))
        o_ref[:, vs] = y.astype(BF16)


def _gla(proj, ga, wup, b_alpha, gla_norm):
    S = proj.shape[0]
    R = GLA_ROWS
    return pl.pallas_call(
        _gla_kernel,
        grid=(S // R,),
        in_specs=[
            pl.BlockSpec((R, GLA_QK), lambda i: (i, COL_GQ // GLA_QK)),
            pl.BlockSpec((R, GLA_QK), lambda i: (i, COL_GK // GLA_QK)),
            pl.BlockSpec((R, GLA_V), lambda i: (i, COL_GV // GLA_V)),
            pl.BlockSpec((R, GLA_V), lambda i: (i, COL_GG // GLA_V)),
            pl.BlockSpec((R, V7X_LANES), lambda i: (i, 0)),
            pl.BlockSpec((V7X_LANES, GLA_QK), lambda i: (0, 0)),
            pl.BlockSpec((1, GLA_QK), lambda i: (0, 0)),
            pl.BlockSpec((1, GLA_DV), lambda i: (0, 0)),
        ],
        out_specs=pl.BlockSpec((R, GLA_V), lambda i: (i, 0)),
        out_shape=jax.ShapeDtypeStruct((S, GLA_V), BF16),
        scratch_shapes=[
            pltpu.VMEM((GLA_HEADS, GLA_DV, GLA_DK), F32),
            pltpu.VMEM((R, GLA_QK), F32),
            pltpu.VMEM((GLA_HEADS, R, R), F32),
        ],
        compiler_params=_params(("arbitrary",)),
        name="gla",
    )(proj, proj, proj, proj, ga, wup, b_alpha, gla_norm)


def _attn_kernel(slopes_ref, lq1_ref, lk1_ref, lq2_ref, lk2_ref, gn_ref,
                 q_ref, k_ref, vt_ref, o_ref, feat_scr, kn_scr, run_scr, qa_scr, s_scr, m_scr,
                 l_scr, acc_scr):
    TQ, TK, QW, D = q_ref.shape[0], ATT_T, ATT_QW, DIFF_DQK
    h = pl.program_id(0)
    i = pl.program_id(1)
    n_kb = TQ // TK
    lane = lax.broadcasted_iota(jnp.int32, (TK, V7X_LANES), 1)

    slope2 = slopes_ref[h] * LOG2E
    blk_lane = lax.broadcasted_iota(jnp.int32, (1, V7X_LANES), 1)

    n_blocks = feat_scr.shape[0] // TK

    def fill_block(blk):
        r0 = pl.multiple_of(blk * TK, TK)
        row = lax.broadcasted_iota(jnp.int32, (TK, V7X_LANES), 0)
        b = slope2 * (row + r0).astype(F32)
        hi = b.astype(BF16).astype(F32)
        r1 = b - hi
        mid = r1.astype(BF16).astype(F32)
        lo = r1 - mid
        feat = jnp.where(lane == 0, hi,
                         jnp.where(lane == 1, mid, jnp.where(lane == 2, lo, 0.0)))
        feat_scr[pl.ds(r0, TK), :] = feat.astype(BF16)
        for c in range(2):
            kc = k_ref[pl.ds(r0, TK), c * D:(c + 1) * D].astype(F32)
            sq = jnp.sum(kc * kc, axis=-1, keepdims=True)
            run = jnp.maximum(run_scr[c], jnp.max(sq, axis=0, keepdims=True))
            run_scr[c] = run
            kn_scr[c] = jnp.where(blk_lane == blk, jnp.sqrt(run), kn_scr[c])

    @pl.when(i == 0)
    def _():
        run_scr[...] = jnp.zeros_like(run_scr)
        kn_scr[...] = jnp.zeros_like(kn_scr)
        fill_block(0)
        fill_block(1)

    m_scr[...] = jnp.full_like(m_scr, NEG_BIG)
    l_scr[...] = jnp.zeros_like(l_scr)
    acc_scr[...] = jnp.zeros_like(acc_scr)

    q_lane = lax.broadcasted_iota(jnp.int32, (TQ, V7X_LANES), 1)
    q_ones = jnp.where(q_lane < 3, 1.0, 0.0).astype(BF16)
    for c in range(2):
        qa_scr[c, :, 0:D] = q_ref[:, c * D:(c + 1) * D]
        qa_scr[c, :, D:2 * D] = q_ones
    key_row = lax.broadcasted_iota(jnp.int32, (TK, QW), 0)
    query_col = lax.broadcasted_iota(jnp.int32, (TK, QW), 1)

    def slabs(key_off):
        return [sl for sl in range(TQ // QW)
                if key_off is None or key_off <= sl * QW + QW - 1]

    def scores(j, slot, key_off):
        r0 = pl.multiple_of(j * TK, TK)
        feat = feat_scr[pl.ds(r0, TK), :]
        for c in range(2):
            k = jnp.concatenate([k_ref[pl.ds(r0, TK), c * D:(c + 1) * D], feat], axis=1)
            for sl in slabs(key_off):
                qs = slice(sl * QW, (sl + 1) * QW)
                s_scr[slot, c, :, qs] = lax.dot_general(
                    k, qa_scr[c, qs, :], (((1,), (1,)), ((), ())),
                    preferred_element_type=F32)

    def accumulate(j, slot, key_off):
        vt = vt_ref[j]
        for c in range(2):
            for sl in slabs(key_off):
                q_lo = sl * QW
                qs = slice(q_lo, q_lo + QW)
                s = s_scr[slot, c, :, qs]
                if key_off is not None and key_off + TK - 1 > q_lo:
                    s = jnp.where(key_row + key_off <= query_col + q_lo, s, NEG_BIG)
                m_prev = m_scr[c, :, qs]
                m_new = jnp.maximum(m_prev, jnp.max(s, axis=0, keepdims=True))
                alpha = jnp.exp2(m_prev - m_new)
                p = jnp.exp2(s - m_new)
                l_scr[c, :, qs] = alpha * l_scr[c, :, qs] + jnp.sum(p, axis=0, keepdims=True)
                acc_scr[c, :, qs] = alpha * acc_scr[c, :, qs] + jnp.dot(
                    vt, p.astype(BF16), preferred_element_type=F32)
                m_scr[c, :, qs] = m_new

    assert n_kb == 2
    scores(2 * i, 0, 0)
    scores(2 * i + 1, 1, TK)
    nxt = jnp.minimum(2 * i + 2, n_blocks - 2)
    fill_block(nxt)
    fill_block(nxt + 1)
    accumulate(2 * i + 1, 1, TK)
    scores(jnp.maximum(2 * i - 1, 0), 1, None)
    accumulate(2 * i, 0, 0)

    last_pos = (blk_lane * TK + (TK - 1)).astype(F32)
    skip = blk_lane < 2 * i
    for c in range(2):
        qf = q_ref[:, c * D:(c + 1) * D].astype(F32)
        qn = jnp.sqrt(jnp.max(jnp.sum(qf * qf, axis=-1, keepdims=True), axis=0, keepdims=True))
        bound = qn * kn_scr[c] * ATT_NORM_SLACK + slope2 * last_pos
        m_min = jnp.min(m_scr[c], axis=-1, keepdims=True)
        skip = skip & (bound < m_min - ATT_SKIP_LOG2)
    first_kept = jnp.min(jnp.where(skip, float(V7X_LANES), blk_lane.astype(F32)))
    n_skip = jnp.minimum(first_kept.astype(jnp.int32), 2 * i)
    n_pairs = i - n_skip // 2

    def pair(lo, prefetch):
        scores(lo, 0, None)
        accumulate(lo + 1, 1, None)
        if prefetch:
            scores(lo - 1, 1, None)
        accumulate(lo, 0, None)

    def body(t, carry):
        pair(2 * (i - 1 - t), True)
        return carry

    lax.fori_loop(0, n_pairs - 1, body, 0)

    @pl.when(n_pairs > 0)
    def _():
        pair(2 * (i - n_pairs), False)

    lam = (jnp.exp(jnp.sum(lq1_ref[...] * lk1_ref[...], axis=-1, keepdims=True))
           - jnp.exp(jnp.sum(lq2_ref[...] * lk2_ref[...], axis=-1, keepdims=True))
           + LAMBDA_INIT)
    o1 = acc_scr[0] / l_scr[0]
    o2 = acc_scr[1] / l_scr[1]
    o = (o1 - lam * o2).T
    o_ref[...] = (_rms(o, gn_ref[...]) * (1.0 - LAMBDA_INIT)).astype(BF16)


def _diff_attn(proj, vt, slopes, lq1, lk1, lq2, lk2, diff_norm):
    S = proj.shape[0]
    TQ = min(ATT_TQ, S)
    W = 2 * DIFF_DQK
    vec = pl.BlockSpec((1, DIFF_DQK), lambda h, i: (0, 0))
    return pl.pallas_call(
        _attn_kernel,
        grid=(DIFF_HEADS, S // TQ),
        in_specs=[
            pl.BlockSpec(memory_space=pltpu.SMEM),
            vec, vec, vec, vec,
            pl.BlockSpec((1, DIFF_DV), lambda h, i: (0, 0)),
            pl.BlockSpec((TQ, W), lambda h, i: (i, COL_DQ // W + h)),
            pl.BlockSpec((S, W), lambda h, i: (0, COL_DK // W + h)),
            pl.BlockSpec((S // ATT_T, DIFF_DV, ATT_T), lambda h, i: (0, h, 0)),
        ],
        out_specs=pl.BlockSpec((TQ, DIFF_DV), lambda h, i: (i, h)),
        out_shape=jax.ShapeDtypeStruct((S, DIFF_V), BF16),
        scratch_shapes=[
            pltpu.VMEM((S, V7X_LANES), BF16),
            pltpu.VMEM((2, 1, V7X_LANES), F32),
            pltpu.VMEM((2, 1, V7X_LANES), F32),
            pltpu.VMEM((2, TQ, W), BF16),
            pltpu.VMEM((2, 2, ATT_T, TQ), F32),
            pltpu.VMEM((2, 1, TQ), F32),
            pltpu.VMEM((2, 1, TQ), F32),
            pltpu.VMEM((2, DIFF_DV, TQ), F32),
        ],
        compiler_params=_params(("arbitrary", "arbitrary")),
        name="diff_attn",
    )(slopes, lq1, lk1, lq2, lk2, diff_norm, proj, proj, vt)


def _out_proj_kernel(oa_ref, ob_ref, wo_ref, x_ref, gpost_ref, gpre_ref,
                     x1_ref, h2_ref):
    for r0 in range(0, x_ref.shape[0], OUT_RH):
        rs = slice(r0, r0 + OUT_RH)
        m = (jnp.dot(oa_ref[rs, :], wo_ref[0:GLA_V, :], preferred_element_type=F32)
             + jnp.dot(ob_ref[rs, :], wo_ref[GLA_V:GLA_V + DIFF_V, :],
                       preferred_element_type=F32))
        x1 = x_ref[rs, :] + _rms(m, gpost_ref[...])
        x1_ref[rs, :] = x1
        h2_ref[rs, :] = _rms(x1, gpre_ref[...]).astype(BF16)


def _out_proj(o_a, o_b, w_o, x, g_post, g_pre):
    S = x.shape[0]
    tm = min(512, S)
    row = lambda i: (i, 0)
    fixed = lambda i: (0, 0)
    return pl.pallas_call(
        _out_proj_kernel,
        grid=(S // tm,),
        in_specs=[
            pl.BlockSpec((tm, GLA_V), row),
            pl.BlockSpec((tm, DIFF_V), row),
            pl.BlockSpec((GLA_V + DIFF_V, D_MODEL), fixed),
            pl.BlockSpec((tm, D_MODEL), row),
            pl.BlockSpec((1, D_MODEL), fixed),
            pl.BlockSpec((1, D_MODEL), fixed),
        ],
        out_specs=[
            pl.BlockSpec((tm, D_MODEL), row),
            pl.BlockSpec((tm, D_MODEL), row),
        ],
        out_shape=[
            jax.ShapeDtypeStruct((S, D_MODEL), F32),
            jax.ShapeDtypeStruct((S, D_MODEL), BF16),
        ],
        compiler_params=_params(("parallel",)),
        name="out_proj",
    )(o_a, o_b, w_o, x, g_post, g_pre)


def _gelu_tanh(x):
    k = -2.0 * math.sqrt(2.0 / math.pi) * LOG2E
    return x / (1.0 + jnp.exp2(x * (k + (k * 0.044715) * (x * x))))


def _ffn_kernel(h2_ref, wa_ref, wb_ref, cw_ref, cb_ref, wout_ref, x1_ref, gpost_ref,
                o_ref, acc_scr, abuf_scr, halo_scr, *, tm):
    i = pl.program_id(0)
    j = pl.program_id(1)
    nj = pl.num_programs(1)
    P = V7X_SUBLANES

    @pl.when((i == 0) & (j == 0))
    def _():
        acc_scr[...] = jnp.zeros_like(acc_scr)
        halo_scr[...] = jnp.zeros_like(halo_scr)

    abuf_scr[0:P, :] = halo_scr[j]
    for r0 in range(0, tm, FFN_RH):
        rs = slice(r0, r0 + FFN_RH)
        h2 = h2_ref[rs, :]
        contrib = None
        for c0 in range(0, wa_ref.shape[1], FFN_CW):
            cs = slice(c0, c0 + FFN_CW)
            a = jnp.dot(h2, wa_ref[:, cs], preferred_element_type=F32)
            b = jnp.dot(h2, wb_ref[:, cs], preferred_element_type=F32)
            abuf_scr[P + r0:P + r0 + FFN_RH, cs] = a
            a1 = abuf_scr[P - 1 + r0:P - 1 + r0 + FFN_RH, cs]
            a2 = abuf_scr[P - 2 + r0:P - 2 + r0 + FFN_RH, cs]
            conv = (cw_ref[0:1, cs] * a2 + cw_ref[1:2, cs] * a1 + cw_ref[2:3, cs] * a
                    + cb_ref[:, cs])
            g = (_gelu_tanh(conv) * b).astype(BF16)
            part = jnp.dot(g, wout_ref[cs, :], preferred_element_type=F32)
            contrib = part if contrib is None else contrib + part
        acc_scr[rs, :] += contrib
    halo_scr[j] = abuf_scr[tm:tm + P, :]

    @pl.when(j == nj - 1)
    def _():
        o_ref[...] = x1_ref[...] + _rms(acc_scr[...], gpost_ref[...])
        acc_scr[...] = jnp.zeros_like(acc_scr)


def _ffn(h2, w_ffn_in, conv_w, conv_b, w_ffn_out, x1, g_post):
    S = h2.shape[0]
    tm = min(512, S)
    tn = 512
    nj = D_FF // tn
    row = lambda i, j: (i, 0)
    return pl.pallas_call(
        functools.partial(_ffn_kernel, tm=tm),
        grid=(S // tm, nj),
        in_specs=[
            pl.BlockSpec((tm, D_MODEL), row),
            pl.BlockSpec((D_MODEL, tn), lambda i, j: (0, j)),
            pl.BlockSpec((D_MODEL, tn), lambda i, j: (0, nj + j)),
            pl.BlockSpec((CONV_W, tn), lambda i, j: (0, j)),
            pl.BlockSpec((1, tn), lambda i, j: (0, j)),
            pl.BlockSpec((tn, D_MODEL), lambda i, j: (j, 0)),
            pl.BlockSpec((tm, D_MODEL), row),
            pl.BlockSpec((1, D_MODEL), lambda i, j: (0, 0)),
        ],
        out_specs=pl.BlockSpec((tm, D_MODEL), row),
        out_shape=jax.ShapeDtypeStruct((S, D_MODEL), F32),
        scratch_shapes=[
            pltpu.VMEM((tm, D_MODEL), F32),
            pltpu.VMEM((tm + V7X_SUBLANES, tn), F32),
            pltpu.VMEM((nj, V7X_SUBLANES, tn), F32),
        ],
        compiler_params=_params(("arbitrary", "arbitrary")),
        name="ffn",
    )(h2, w_ffn_in, w_ffn_in, conv_w, conv_b, w_ffn_out, x1, g_post)


def _layer(x, attn_pre_norm, w_in, w_alpha_up, b_alpha, gla_norm, lambda_q1, lambda_k1,
           lambda_q2, lambda_k2, diff_norm, w_o, attn_post_norm, ffn_pre_norm, w_ffn_in,
           conv_w, conv_b, w_ffn_out, ffn_post_norm):
    vec = lambda p: p.reshape(1, -1).astype(F32)

    w_in16 = w_in.astype(BF16)
    w_main = jnp.concatenate(
        [w_in16[:, :GA_OFFSET], w_in16[:, GA_OFFSET + GLA_RANK:]], axis=1)
    w_ga = jnp.pad(w_in16[:, GA_OFFSET:GA_OFFSET + GLA_RANK],
                   ((0, 0), (0, V7X_LANES - GLA_RANK)))
    colscale = jnp.concatenate([
        jnp.full((GLA_QK,), GLA_DK ** -0.5, F32),
        jnp.ones((COL_DQ - COL_GK,), F32),
        jnp.full((DIFF_QK,), DIFF_DQK ** -0.5 * LOG2E, F32),
        jnp.ones((PROJ_COLS - COL_DK,), F32),
    ]).reshape(1, PROJ_COLS)
    wup = jnp.pad(w_alpha_up, ((0, V7X_LANES - GLA_RANK), (0, 0))).astype(BF16)
    slopes = jnp.asarray(
        [2.0 ** (-8.0 * (h + 1) / DIFF_HEADS) for h in range(DIFF_HEADS)], F32)

    proj, ga, vt = _in_proj(x, vec(attn_pre_norm), w_main, w_ga, colscale)
    o_a = _gla(proj, ga, wup, vec(b_alpha), vec(gla_norm))
    o_b = _diff_attn(proj, vt, slopes, vec(lambda_q1), vec(lambda_k1), vec(lambda_q2),
                     vec(lambda_k2), vec(diff_norm))
    x1, h2 = _out_proj(o_a, o_b, w_o.astype(BF16), x, vec(attn_post_norm),
                       vec(ffn_pre_norm))
    return _ffn(h2, w_ffn_in.astype(BF16), conv_w.astype(F32), vec(conv_b),
                w_ffn_out.astype(BF16), x1, vec(ffn_post_norm))


def kernel(x, attn_pre_norm, w_in, w_alpha_up, b_alpha, gla_norm, lambda_q1, lambda_k1,
           lambda_q2, lambda_k2, diff_norm, w_o, attn_post_norm, ffn_pre_norm, w_ffn_in,
           conv_w, conv_b, w_ffn_out, ffn_post_norm):
    B = x.shape[0]
    depth = w_in.shape[0]
    assert depth == 1, "lambda_init is baked for a single layer"
    outs = []
    for bi in range(B):
        xb = x[bi]
        for l in range(depth):
            xb = _layer(xb, attn_pre_norm[l], w_in[l], w_alpha_up[l], b_alpha[l], gla_norm[l],
                        lambda_q1[l], lambda_k1[l], lambda_q2[l], lambda_k2[l], diff_norm[l],
                        w_o[l], attn_post_norm[l], ffn_pre_norm[l], w_ffn_in[l], conv_w[l],
                        conv_b[l], w_ffn_out[l], ffn_post_norm[l])
        outs.append(xb)
    return outs[0][None] if B == 1 else jnp.stack(outs, axis=0)
```

```python
import functools
import math

import jax
import jax.numpy as jnp
from jax import lax
from jax.experimental import pallas as pl
from jax.experimental.pallas import tpu as pltpu

F32 = jnp.float32
BF16 = jnp.bfloat16

D_MODEL = 2048
GLA_HEADS = 4
GLA_DK = 128
GLA_DV = 256
GLA_RANK = 16
GLA_TAU = 16.0
DIFF_HEADS = 4
DIFF_DQK = 128
DIFF_DV = 256
D_FF = 5632
CONV_W = 3
EPS = 1e-6
LAMBDA_INIT = 0.8 - 0.6 * math.exp(-0.3 * 0)
LOG2E = math.log2(math.e)

GLA_QK = GLA_HEADS * GLA_DK
GLA_V = GLA_HEADS * GLA_DV
DIFF_QK = DIFF_HEADS * 2 * DIFF_DQK
DIFF_V = DIFF_HEADS * DIFF_DV

COL_GQ = 0
COL_GK = COL_GQ + GLA_QK
COL_GV = COL_GK + GLA_QK
COL_GG = COL_GV + GLA_V
COL_DQ = COL_GG + GLA_V
COL_DK = COL_DQ + DIFF_QK
COL_DV = COL_DK + DIFF_QK
PROJ_COLS = COL_DV + DIFF_V
GA_OFFSET = GLA_QK + GLA_QK + GLA_V + GLA_V

V7X_LANES = 128
V7X_SUBLANES = 8
V7X_VMEM_LIMIT_BYTES = 56 * 1024 * 1024

IN_TN = 1536
OUT_RH = 256
FFN_CW = 512
FFN_RH = 512
ATT_T = 512
ATT_TQ = 1024
ATT_QW = 256
ATT_SKIP_LOG2 = 160.0
ATT_NORM_SLACK = 1.001
GLA_CHUNK = 64
GLA_ROWS = 256
GLA_SUB = 16
GLA_SAFE_SPAN = 60.0
NEG_BIG = -1e30


def _rms(v, g):
    return v * lax.rsqrt(jnp.mean(v * v, axis=-1, keepdims=True) + EPS) * g


def _params(dims):
    return pltpu.CompilerParams(dimension_semantics=dims,
                                vmem_limit_bytes=V7X_VMEM_LIMIT_BYTES)


def _in_proj_kernel(x_ref, g_ref, w_ref, wga_ref, cs_ref, o_ref, ga_ref, vt_ref, h_scr,
                    *, tn):
    j = pl.program_id(1)

    @pl.when(j == 0)
    def _():
        h = _rms(x_ref[...], g_ref[...]).astype(BF16)
        h_scr[...] = h
        ga_ref[...] = jnp.dot(h, wga_ref[...], preferred_element_type=F32)

    acc = jnp.dot(h_scr[...], w_ref[...], preferred_element_type=F32)
    o_ref[...] = (acc * cs_ref[...]).astype(BF16)

    @pl.when(j == COL_DV // tn)
    def _():
        v0 = COL_DV % tn
        acc_t = acc[:, v0:v0 + DIFF_V].T.astype(BF16)
        for kb in range(vt_ref.shape[0]):
            vt_ref[kb] = acc_t[:, kb * ATT_T:(kb + 1) * ATT_T]


def _in_proj(x, g, w_main, w_ga, colscale):
    S = x.shape[0]
    tm = min(1024, S)
    tn = IN_TN
    return pl.pallas_call(
        functools.partial(_in_proj_kernel, tn=tn),
        grid=(S // tm, PROJ_COLS // tn),
        in_specs=[
            pl.BlockSpec((tm, D_MODEL), lambda i, j: (i, 0)),
            pl.BlockSpec((1, D_MODEL), lambda i, j: (0, 0)),
            pl.BlockSpec((D_MODEL, tn), lambda i, j: (0, j)),
            pl.BlockSpec((D_MODEL, V7X_LANES), lambda i, j: (0, 0)),
            pl.BlockSpec((1, tn), lambda i, j: (0, j)),
        ],
        out_specs=[
            pl.BlockSpec((tm, tn), lambda i, j: (i, j)),
            pl.BlockSpec((tm, V7X_LANES), lambda i, j: (i, 0)),
            pl.BlockSpec((tm // ATT_T, DIFF_V, ATT_T), lambda i, j: (i, 0, 0)),
        ],
        out_shape=[
            jax.ShapeDtypeStruct((S, PROJ_COLS), BF16),
            jax.ShapeDtypeStruct((S, V7X_LANES), F32),
            jax.ShapeDtypeStruct((S // ATT_T, DIFF_V, ATT_T), BF16),
        ],
        scratch_shapes=[pltpu.VMEM((tm, D_MODEL), BF16)],
        compiler_params=_params(("parallel", "arbitrary")),
        name="in_proj",
    )(x, g, w_main, w_ga, colscale)


def _log_sigmoid(x):
    return jnp.minimum(x, 0.0) - jnp.log(1.0 + jnp.exp(-jnp.abs(x)))


def _gla_kernel(q_ref, k_ref, v_ref, gate_ref, ga_ref, gan_ref, wup_ref, ba_ref, gn_ref,
                o_ref, st_scr, b_scr, d_scr, bn_scr, dn_scr, span_scr, a_scr):
    C, R = GLA_CHUNK, GLA_ROWS
    n_chunks = R // C

    row = lax.broadcasted_iota(jnp.int32, (R, R), 0)
    col = lax.broadcasted_iota(jnp.int32, (R, R), 1)
    shift = C.bit_length() - 1
    in_chunk = (col <= row) & ((row >> shift) == (col >> shift))
    tril = jnp.where(in_chunk, 1.0, 0.0).astype(BF16)
    gn = gn_ref[...]

    def decay_prefix(ga_blk):
        z = jnp.dot(ga_blk.astype(BF16), wup_ref[...], preferred_element_type=F32) + ba_ref[...]
        la = _log_sigmoid(z) * (1.0 / GLA_TAU)
        la_hi = la.astype(BF16)
        la_lo = (la - la_hi.astype(F32)).astype(BF16)
        b_all = (jnp.dot(tril, la_hi, preferred_element_type=F32)
                 + jnp.dot(tril, la_lo, preferred_element_type=F32))
        b_first = jnp.concatenate(
            [jnp.broadcast_to(b_all[c * C:c * C + 1], (C, GLA_QK)) for c in range(n_chunks)],
            axis=0)
        d = b_first - b_all
        bn_scr[...] = b_all
        dn_scr[...] = d
        span_scr[0] = jnp.max(d)

    @pl.when(pl.program_id(0) == 0)
    def _():
        st_scr[...] = jnp.zeros_like(st_scr)
        decay_prefix(ga_ref[...])

    b_scr[...] = bn_scr[...]
    d_scr[...] = dn_scr[...]
    small_span = span_scr[0] <= GLA_SAFE_SPAN

    def heads():
        for h in range(GLA_HEADS):
            ks = slice(h * GLA_DK, (h + 1) * GLA_DK)
            yield h, ks, q_ref[:, ks].astype(F32), k_ref[:, ks].astype(F32)

    @pl.when(small_span)
    def _():
        for h, ks, q, k in heads():
            d = d_scr[:, ks]
            q_t = (q * jnp.exp(-d)).astype(BF16)
            k_t = (k * jnp.exp(d)).astype(BF16)
            s = lax.dot_general(q_t, k_t, (((1,), (1,)), ((), ())),
                                preferred_element_type=F32)
            a_scr[h] = jnp.where(in_chunk, s, 0.0)

    @pl.when(jnp.logical_not(small_span))
    def _():
        a_scr[...] = jnp.zeros_like(a_scr)
        sub_row = lax.broadcasted_iota(jnp.int32, (GLA_SUB, C), 0)
        sub_col = lax.broadcasted_iota(jnp.int32, (GLA_SUB, C), 1)
        for h, ks, q_all, k_all in heads():
            for c in range(n_chunks):
                q = q_all[c * C:(c + 1) * C]
                k = k_all[c * C:(c + 1) * C]
                b = b_scr[c * C:(c + 1) * C, ks]
                for blk in range(C // GLA_SUB):
                    r0 = blk * GLA_SUB
                    b_blk = b[r0:r0 + GLA_SUB]
                    q_blk = q[r0:r0 + GLA_SUB]
                    b_ref0 = b[r0:r0 + 1]
                    a_blk = jnp.zeros((GLA_SUB, C), F32)
                    if blk > 0:
                        q_t = (q_blk * jnp.exp(b_blk - b_ref0)).astype(BF16)
                        k_t = (k * jnp.exp(jnp.minimum(b_ref0 - b, 0.0))).astype(BF16)
                        off = lax.dot_general(q_t, k_t, (((1,), (1,)), ((), ())),
                                              preferred_element_type=F32)
                        a_blk = jnp.where(sub_col < r0, off, 0.0)
                    for jj in range(GLA_SUB):
                        kj = k[r0 + jj:r0 + jj + 1]
                        bj = b[r0 + jj:r0 + jj + 1]
                        t = q_blk * kj * jnp.exp(jnp.minimum(b_blk - bj, 0.0))
                        cj = jnp.sum(t, axis=-1, keepdims=True)
                        a_blk = jnp.where((sub_col == r0 + jj) & (sub_row >= jj), cj, a_blk)
                    a_scr[h, c * C + r0:c * C + r0 + GLA_SUB, c * C:(c + 1) * C] = a_blk

    for h, ks, q, k in heads():
        vs = slice(h * GLA_DV, (h + 1) * GLA_DV)
        v = v_ref[:, vs]
        b = b_scr[:, ks]
        q_in = (q * jnp.exp(b)).astype(BF16)
        st = st_scr[h]
        o_inter = []
        for c in range(n_chunks):
            cr = slice(c * C, (c + 1) * C)
            o_inter.append(lax.dot_general(q_in[cr], st.astype(BF16), (((1,), (1,)), ((), ())),
                                           preferred_element_type=F32))
            b_last = b[(c + 1) * C - 1:(c + 1) * C]
            k_out = (k[cr] * jnp.exp(b_last - b[cr])).astype(BF16)
            upd = lax.dot_general(v[cr], k_out, (((0,), (0,)), ((), ())),
                                  preferred_element_type=F32)
            st = st * jnp.exp(b_last) + upd
        st_scr[h] = st
        o = (jnp.concatenate(o_inter, axis=0)
             + jnp.dot(a_scr[h].astype(BF16), v, preferred_element_type=F32))

        gate = gate_ref[:, vs].astype(F32)
        y = _rms(o, gn) * (gate * jax.nn.sigmoid(gate))
        o_ref[:, vs] = y.astype(BF16)

    decay_prefix(gan_ref[...])


def _gla(proj, ga, wup, b_alpha, gla_norm):
    S = proj.shape[0]
    R = GLA_ROWS
    return pl.pallas_call(
        _gla_kernel,
        grid=(S // R,),
        in_specs=[
            pl.BlockSpec((R, GLA_QK), lambda i: (i, COL_GQ // GLA_QK)),
            pl.BlockSpec((R, GLA_QK), lambda i: (i, COL_GK // GLA_QK)),
            pl.BlockSpec((R, GLA_V), lambda i: (i, COL_GV // GLA_V)),
            pl.BlockSpec((R, GLA_V), lambda i: (i, COL_GG // GLA_V)),
            pl.BlockSpec((R, V7X_LANES), lambda i: (i, 0)),
            pl.BlockSpec((R, V7X_LANES), lambda i: (jnp.minimum(i + 1, S // R - 1), 0)),
            pl.BlockSpec((V7X_LANES, GLA_QK), lambda i: (0, 0)),
            pl.BlockSpec((1, GLA_QK), lambda i: (0, 0)),
            pl.BlockSpec((1, GLA_DV), lambda i: (0, 0)),
        ],
        out_specs=pl.BlockSpec((R, GLA_V), lambda i: (i, 0)),
        out_shape=jax.ShapeDtypeStruct((S, GLA_V), BF16),
        scratch_shapes=[
            pltpu.VMEM((GLA_HEADS, GLA_DV, GLA_DK), F32),
            pltpu.VMEM((R, GLA_QK), F32),
            pltpu.VMEM((R, GLA_QK), F32),
            pltpu.VMEM((R, GLA_QK), F32),
            pltpu.VMEM((R, GLA_QK), F32),
            pltpu.SMEM((1,), F32),
            pltpu.VMEM((GLA_HEADS, R, R), F32),
        ],
        compiler_params=_params(("arbitrary",)),
        name="gla",
    )(proj, proj, proj, proj, ga, ga, wup, b_alpha, gla_norm)


def _attn_kernel(slopes_ref, lq1_ref, lk1_ref, lq2_ref, lk2_ref, gn_ref,
                 q_ref, k_ref, vt_ref, o_ref, feat_scr, kn_scr, run_scr, qa_scr, s_scr, m_scr,
                 l_scr, acc_scr):
    TQ, TK, QW, D = q_ref.shape[0], ATT_T, ATT_QW, DIFF_DQK
    h = pl.program_id(0)
    i = pl.program_id(1)
    n_kb = TQ // TK
    lane = lax.broadcasted_iota(jnp.int32, (TK, V7X_LANES), 1)

    slope2 = slopes_ref[h] * LOG2E
    blk_lane = lax.broadcasted_iota(jnp.int32, (1, V7X_LANES), 1)

    n_blocks = feat_scr.shape[0] // TK

    def fill_block(blk):
        r0 = pl.multiple_of(blk * TK, TK)
        row = lax.broadcasted_iota(jnp.int32, (TK, V7X_LANES), 0)
        b = slope2 * (row + r0).astype(F32)
        hi = b.astype(BF16).astype(F32)
        r1 = b - hi
        mid = r1.astype(BF16).astype(F32)
        lo = r1 - mid
        feat = jnp.where(lane == 0, hi,
                         jnp.where(lane == 1, mid, jnp.where(lane == 2, lo, 0.0)))
        feat_scr[pl.ds(r0, TK), :] = feat.astype(BF16)
        for c in range(2):
            kc = k_ref[pl.ds(r0, TK), c * D:(c + 1) * D].astype(F32)
            sq = jnp.sum(kc * kc, axis=-1, keepdims=True)
            run = jnp.maximum(run_scr[c], jnp.max(sq, axis=0, keepdims=True))
            run_scr[c] = run
            kn_scr[c] = jnp.where(blk_lane == blk, jnp.sqrt(run), kn_scr[c])

    @pl.when(i == 0)
    def _():
        run_scr[...] = jnp.zeros_like(run_scr)
        kn_scr[...] = jnp.zeros_like(kn_scr)
        fill_block(0)
        fill_block(1)

    m_scr[...] = jnp.full_like(m_scr, NEG_BIG)
    l_scr[...] = jnp.zeros_like(l_scr)
    acc_scr[...] = jnp.zeros_like(acc_scr)

    q_lane = lax.broadcasted_iota(jnp.int32, (TQ, V7X_LANES), 1)
    q_ones = jnp.where(q_lane < 3, 1.0, 0.0).astype(BF16)
    for c in range(2):
        qa_scr[c, :, 0:D] = q_ref[:, c * D:(c + 1) * D]
        qa_scr[c, :, D:2 * D] = q_ones
    key_row = lax.broadcasted_iota(jnp.int32, (TK, QW), 0)
    query_col = lax.broadcasted_iota(jnp.int32, (TK, QW), 1)

    def slabs(key_off):
        return [sl for sl in range(TQ // QW)
                if key_off is None or key_off <= sl * QW + QW - 1]

    def scores(j, slot, key_off):
        r0 = pl.multiple_of(j * TK, TK)
        feat = feat_scr[pl.ds(r0, TK), :]
        for c in range(2):
            k = jnp.concatenate([k_ref[pl.ds(r0, TK), c * D:(c + 1) * D], feat], axis=1)
            for sl in slabs(key_off):
                qs = slice(sl * QW, (sl + 1) * QW)
                s_scr[slot, c, :, qs] = lax.dot_general(
                    k, qa_scr[c, qs, :], (((1,), (1,)), ((), ())),
                    preferred_element_type=F32)

    def accumulate(j, slot, key_off):
        vt = vt_ref[j]
        for c in range(2):
            for sl in slabs(key_off):
                q_lo = sl * QW
                qs = slice(q_lo, q_lo + QW)
                s = s_scr[slot, c, :, qs]
                if key_off is not None and key_off + TK - 1 > q_lo:
                    s = jnp.where(key_row + key_off <= query_col + q_lo, s, NEG_BIG)
                m_prev = m_scr[c, :, qs]
                m_new = jnp.maximum(m_prev, jnp.max(s, axis=0, keepdims=True))
                alpha = jnp.exp2(m_prev - m_new)
                p = jnp.exp2(s - m_new)
                l_scr[c, :, qs] = alpha * l_scr[c, :, qs] + jnp.sum(p, axis=0, keepdims=True)
                acc_scr[c, :, qs] = alpha * acc_scr[c, :, qs] + jnp.dot(
                    vt, p.astype(BF16), preferred_element_type=F32)
                m_scr[c, :, qs] = m_new

    assert n_kb == 2
    scores(2 * i, 0, 0)
    scores(2 * i + 1, 1, TK)
    nxt = jnp.minimum(2 * i + 2, n_blocks - 2)
    fill_block(nxt)
    fill_block(nxt + 1)
    accumulate(2 * i + 1, 1, TK)
    scores(jnp.maximum(2 * i - 1, 0), 1, None)
    accumulate(2 * i, 0, 0)

    last_pos = (blk_lane * TK + (TK - 1)).astype(F32)
    skip = blk_lane < 2 * i
    for c in range(2):
        qf = q_ref[:, c * D:(c + 1) * D].astype(F32)
        qn = jnp.sqrt(jnp.max(jnp.sum(qf * qf, axis=-1, keepdims=True), axis=0, keepdims=True))
        bound = qn * kn_scr[c] * ATT_NORM_SLACK + slope2 * last_pos
        m_min = jnp.min(m_scr[c], axis=-1, keepdims=True)
        skip = skip & (bound < m_min - ATT_SKIP_LOG2)
    first_kept = jnp.min(jnp.where(skip, float(V7X_LANES), blk_lane.astype(F32)))
    n_skip = jnp.minimum(first_kept.astype(jnp.int32), 2 * i)
    n_pairs = i - n_skip // 2

    def pair(lo, prefetch):
        scores(lo, 0, None)
        accumulate(lo + 1, 1, None)
        if prefetch:
            scores(lo - 1, 1, None)
        accumulate(lo, 0, None)

    def body(t, carry):
        pair(2 * (i - 1 - t), True)
        return carry

    lax.fori_loop(0, n_pairs - 1, body, 0)

    @pl.when(n_pairs > 0)
    def _():
        pair(2 * (i - n_pairs), False)

    lam = (jnp.exp(jnp.sum(lq1_ref[...] * lk1_ref[...], axis=-1, keepdims=True))
           - jnp.exp(jnp.sum(lq2_ref[...] * lk2_ref[...], axis=-1, keepdims=True))
           + LAMBDA_INIT)
    o1 = acc_scr[0] / l_scr[0]
    o2 = acc_scr[1] / l_scr[1]
    o = (o1 - lam * o2).T
    o_ref[...] = (_rms(o, gn_ref[...]) * (1.0 - LAMBDA_INIT)).astype(BF16)


def _diff_attn(proj, vt, slopes, lq1, lk1, lq2, lk2, diff_norm):
    S = proj.shape[0]
    TQ = min(ATT_TQ, S)
    W = 2 * DIFF_DQK
    vec = pl.BlockSpec((1, DIFF_DQK), lambda h, i: (0, 0))
    return pl.pallas_call(
        _attn_kernel,
        grid=(DIFF_HEADS, S // TQ),
        in_specs=[
            pl.BlockSpec(memory_space=pltpu.SMEM),
            vec, vec, vec, vec,
            pl.BlockSpec((1, DIFF_DV), lambda h, i: (0, 0)),
            pl.BlockSpec((TQ, W), lambda h, i: (i, COL_DQ // W + h)),
            pl.BlockSpec((S, W), lambda h, i: (0, COL_DK // W + h)),
            pl.BlockSpec((S // ATT_T, DIFF_DV, ATT_T), lambda h, i: (0, h, 0)),
        ],
        out_specs=pl.BlockSpec((TQ, DIFF_DV), lambda h, i: (i, h)),
        out_shape=jax.ShapeDtypeStruct((S, DIFF_V), BF16),
        scratch_shapes=[
            pltpu.VMEM((S, V7X_LANES), BF16),
            pltpu.VMEM((2, 1, V7X_LANES), F32),
            pltpu.VMEM((2, 1, V7X_LANES), F32),
            pltpu.VMEM((2, TQ, W), BF16),
            pltpu.VMEM((2, 2, ATT_T, TQ), F32),
            pltpu.VMEM((2, 1, TQ), F32),
            pltpu.VMEM((2, 1, TQ), F32),
            pltpu.VMEM((2, DIFF_DV, TQ), F32),
        ],
        compiler_params=_params(("arbitrary", "arbitrary")),
        name="diff_attn",
    )(slopes, lq1, lk1, lq2, lk2, diff_norm, proj, proj, vt)


def _out_proj_kernel(oa_ref, ob_ref, wo_ref, x_ref, gpost_ref, gpre_ref,
                     x1_ref, h2_ref):
    for r0 in range(0, x_ref.shape[0], OUT_RH):
        rs = slice(r0, r0 + OUT_RH)
        m = (jnp.dot(oa_ref[rs, :], wo_ref[0:GLA_V, :], preferred_element_type=F32)
             + jnp.dot(ob_ref[rs, :], wo_ref[GLA_V:GLA_V + DIFF_V, :],
                       preferred_element_type=F32))
        x1 = x_ref[rs, :] + _rms(m, gpost_ref[...])
        x1_ref[rs, :] = x1
        h2_ref[rs, :] = _rms(x1, gpre_ref[...]).astype(BF16)


def _out_proj(o_a, o_b, w_o, x, g_post, g_pre):
    S = x.shape[0]
    tm = min(512, S)
    row = lambda i: (i, 0)
    fixed = lambda i: (0, 0)
    return pl.pallas_call(
        _out_proj_kernel,
        grid=(S // tm,),
        in_specs=[
            pl.BlockSpec((tm, GLA_V), row),
            pl.BlockSpec((tm, DIFF_V), row),
            pl.BlockSpec((GLA_V + DIFF_V, D_MODEL), fixed),
            pl.BlockSpec((tm, D_MODEL), row),
            pl.BlockSpec((1, D_MODEL), fixed),
            pl.BlockSpec((1, D_MODEL), fixed),
        ],
        out_specs=[
            pl.BlockSpec((tm, D_MODEL), row),
            pl.BlockSpec((tm, D_MODEL), row),
        ],
        out_shape=[
            jax.ShapeDtypeStruct((S, D_MODEL), F32),
            jax.ShapeDtypeStruct((S, D_MODEL), BF16),
        ],
        compiler_params=_params(("parallel",)),
        name="out_proj",
    )(o_a, o_b, w_o, x, g_post, g_pre)


def _gelu_tanh(x):
    k = -2.0 * math.sqrt(2.0 / math.pi) * LOG2E
    return x / (1.0 + jnp.exp2(x * (k + (k * 0.044715) * (x * x))))


def _ffn_kernel(h2_ref, wa_ref, wb_ref, cw_ref, cb_ref, wout_ref, x1_ref, gpost_ref,
                o_ref, acc_scr, abuf_scr, halo_scr, *, tm):
    i = pl.program_id(0)
    j = pl.program_id(1)
    nj = pl.num_programs(1)
    P = V7X_SUBLANES

    @pl.when((i == 0) & (j == 0))
    def _():
        acc_scr[...] = jnp.zeros_like(acc_scr)
        halo_scr[...] = jnp.zeros_like(halo_scr)

    abuf_scr[0:P, :] = halo_scr[j]
    for r0 in range(0, tm, FFN_RH):
        rs = slice(r0, r0 + FFN_RH)
        h2 = h2_ref[rs, :]
        contrib = None
        for c0 in range(0, wa_ref.shape[1], FFN_CW):
            cs = slice(c0, c0 + FFN_CW)
            a = jnp.dot(h2, wa_ref[:, cs], preferred_element_type=F32)
            b = jnp.dot(h2, wb_ref[:, cs], preferred_element_type=F32)
            abuf_scr[P + r0:P + r0 + FFN_RH, cs] = a
            a1 = abuf_scr[P - 1 + r0:P - 1 + r0 + FFN_RH, cs]
            a2 = abuf_scr[P - 2 + r0:P - 2 + r0 + FFN_RH, cs]
            conv = (cw_ref[0:1, cs] * a2 + cw_ref[1:2, cs] * a1 + cw_ref[2:3, cs] * a
                    + cb_ref[:, cs])
            g = (_gelu_tanh(conv) * b).astype(BF16)
            part = jnp.dot(g, wout_ref[cs, :], preferred_element_type=F32)
            contrib = part if contrib is None else contrib + part
        acc_scr[rs, :] += contrib
    halo_scr[j] = abuf_scr[tm:tm + P, :]

    @pl.when(j == nj - 1)
    def _():
        o_ref[...] = x1_ref[...] + _rms(acc_scr[...], gpost_ref[...])
        acc_scr[...] = jnp.zeros_like(acc_scr)


def _ffn(h2, w_ffn_in, conv_w, conv_b, w_ffn_out, x1, g_post):
    S = h2.shape[0]
    tm = min(512, S)
    tn = 512
    nj = D_FF // tn
    row = lambda i, j: (i, 0)
    return pl.pallas_call(
        functools.partial(_ffn_kernel, tm=tm),
        grid=(S // tm, nj),
        in_specs=[
            pl.BlockSpec((tm, D_MODEL), row),
            pl.BlockSpec((D_MODEL, tn), lambda i, j: (0, j)),
            pl.BlockSpec((D_MODEL, tn), lambda i, j: (0, nj + j)),
            pl.BlockSpec((CONV_W, tn), lambda i, j: (0, j)),
            pl.BlockSpec((1, tn), lambda i, j: (0, j)),
            pl.BlockSpec((tn, D_MODEL), lambda i, j: (j, 0)),
            pl.BlockSpec((tm, D_MODEL), row),
            pl.BlockSpec((1, D_MODEL), lambda i, j: (0, 0)),
        ],
        out_specs=pl.BlockSpec((tm, D_MODEL), row),
        out_shape=jax.ShapeDtypeStruct((S, D_MODEL), F32),
        scratch_shapes=[
            pltpu.VMEM((tm, D_MODEL), F32),
            pltpu.VMEM((tm + V7X_SUBLANES, tn), F32),
            pltpu.VMEM((nj, V7X_SUBLANES, tn), F32),
        ],
        compiler_params=_params(("arbitrary", "arbitrary")),
        name="ffn",
    )(h2, w_ffn_in, w_ffn_in, conv_w, conv_b, w_ffn_out, x1, g_post)


def _layer(x, attn_pre_norm, w_in, w_alpha_up, b_alpha, gla_norm, lambda_q1, lambda_k1,
           lambda_q2, lambda_k2, diff_norm, w_o, attn_post_norm, ffn_pre_norm, w_ffn_in,
           conv_w, conv_b, w_ffn_out, ffn_post_norm):
    vec = lambda p: p.reshape(1, -1).astype(F32)

    w_in16 = w_in.astype(BF16)
    w_main = jnp.concatenate(
        [w_in16[:, :GA_OFFSET], w_in16[:, GA_OFFSET + GLA_RANK:]], axis=1)
    w_ga = jnp.pad(w_in16[:, GA_OFFSET:GA_OFFSET + GLA_RANK],
                   ((0, 0), (0, V7X_LANES - GLA_RANK)))
    colscale = jnp.concatenate([
        jnp.full((GLA_QK,), GLA_DK ** -0.5, F32),
        jnp.ones((COL_DQ - COL_GK,), F32),
        jnp.full((DIFF_QK,), DIFF_DQK ** -0.5 * LOG2E, F32),
        jnp.ones((PROJ_COLS - COL_DK,), F32),
    ]).reshape(1, PROJ_COLS)
    wup = jnp.pad(w_alpha_up, ((0, V7X_LANES - GLA_RANK), (0, 0))).astype(BF16)
    slopes = jnp.asarray(
        [2.0 ** (-8.0 * (h + 1) / DIFF_HEADS) for h in range(DIFF_HEADS)], F32)

    proj, ga, vt = _in_proj(x, vec(attn_pre_norm), w_main, w_ga, colscale)
    o_a = _gla(proj, ga, wup, vec(b_alpha), vec(gla_norm))
    o_b = _diff_attn(proj, vt, slopes, vec(lambda_q1), vec(lambda_k1), vec(lambda_q2),
                     vec(lambda_k2), vec(diff_norm))
    x1, h2 = _out_proj(o_a, o_b, w_o.astype(BF16), x, vec(attn_post_norm),
                       vec(ffn_pre_norm))
    return _ffn(h2, w_ffn_in.astype(BF16), conv_w.astype(F32), vec(conv_b),
                w_ffn_out.astype(BF16), x1, vec(ffn_post_norm))


def kernel(x, attn_pre_norm, w_in, w_alpha_up, b_alpha, gla_norm, lambda_q1, lambda_k1,
           lambda_q2, lambda_k2, diff_norm, w_o, attn_post_norm, ffn_pre_norm, w_ffn_in,
           conv_w, conv_b, w_ffn_out, ffn_post_norm):
    B = x.shape[0]
    depth = w_in.shape[0]
    assert depth == 1, "lambda_init is baked for a single layer"
    outs = []
    for bi in range(B):
        xb = x[bi]
        for l in range(depth):
            xb = _layer(xb, attn_pre_norm[l], w_in[l], w_alpha_up[l], b_alpha[l], gla_norm[l],
                        lambda_q1[l], lambda_k1[l], lambda_q2[l], lambda_k2[l], diff_norm[l],
                        w_o[l], attn_post_norm[l], ffn_pre_norm[l], w_ffn_in[l], conv_w[l],
                        conv_b[l], w_ffn_out[l], ffn_post_norm[l])
        outs.append(xb)
    return outs[0][None] if B == 1 else jnp.stack(outs, axis=0)
```

```python
import functools
import math

import jax
import jax.numpy as jnp
from jax import lax
from jax.experimental import pallas as pl
from jax.experimental.pallas import tpu as pltpu

F32 = jnp.float32
BF16 = jnp.bfloat16

D_MODEL = 2048
GLA_HEADS = 4
GLA_DK = 128
GLA_DV = 256
GLA_RANK = 16
GLA_TAU = 16.0
DIFF_HEADS = 4
DIFF_DQK = 128
DIFF_DV = 256
D_FF = 5632
CONV_W = 3
EPS = 1e-6
LAMBDA_INIT = 0.8 - 0.6 * math.exp(-0.3 * 0)
LOG2E = math.log2(math.e)

GLA_QK = GLA_HEADS * GLA_DK
GLA_V = GLA_HEADS * GLA_DV
DIFF_QK = DIFF_HEADS * 2 * DIFF_DQK
DIFF_V = DIFF_HEADS * DIFF_DV

COL_GQ = 0
COL_GK = COL_GQ + GLA_QK
COL_GV = COL_GK + GLA_QK
COL_GG = COL_GV + GLA_V
COL_DQ = COL_GG + GLA_V
COL_DK = COL_DQ + DIFF_QK
COL_DV = COL_DK + DIFF_QK
PROJ_COLS = COL_DV + DIFF_V
GA_OFFSET = GLA_QK + GLA_QK + GLA_V + GLA_V

V7X_LANES = 128
V7X_SUBLANES = 8
V7X_VMEM_LIMIT_BYTES = 56 * 1024 * 1024

IN_TM = 1024
IN_TN = 1536
OUT_TM = 512
OUT_RH = 256
FFN_TM = 512
FFN_TN = 512
ATT_T = 512
ATT_TQ = 1024
ATT_QW = 256
ATT_SKIP_LOG2 = 160.0
ATT_NORM_SLACK = 1.001
GLA_CHUNK = 64
GLA_ROWS = 256
GLA_SUB = 16
GLA_SAFE_SPAN = 60.0
NEG_BIG = -1e30


def _rms(v, g):
    return v * lax.rsqrt(jnp.mean(v * v, axis=-1, keepdims=True) + EPS) * g


def _params(dims):
    return pltpu.CompilerParams(dimension_semantics=dims,
                                vmem_limit_bytes=V7X_VMEM_LIMIT_BYTES)


def _in_proj_kernel(x_ref, g_ref, w_ref, wga_ref, cs_ref, o_ref, ga_ref, vt_ref, h_scr,
                    *, tn):
    j = pl.program_id(1)

    @pl.when(j == 0)
    def _():
        h = _rms(x_ref[...], g_ref[...]).astype(BF16)
        h_scr[...] = h
        ga_ref[...] = jnp.dot(h, wga_ref[...], preferred_element_type=F32)

    acc = jnp.dot(h_scr[...], w_ref[...], preferred_element_type=F32)
    o_ref[...] = (acc * cs_ref[...]).astype(BF16)

    @pl.when(j == COL_DV // tn)
    def _():
        v0 = COL_DV % tn
        acc_t = acc[:, v0:v0 + DIFF_V].T.astype(BF16)
        for kb in range(vt_ref.shape[0]):
            vt_ref[kb] = acc_t[:, kb * ATT_T:(kb + 1) * ATT_T]


def _in_proj(x, g, w_main, w_ga, colscale):
    S = x.shape[0]
    tm = min(IN_TM, S)
    tn = IN_TN
    return pl.pallas_call(
        functools.partial(_in_proj_kernel, tn=tn),
        grid=(S // tm, PROJ_COLS // tn),
        in_specs=[
            pl.BlockSpec((tm, D_MODEL), lambda i, j: (i, 0)),
            pl.BlockSpec((1, D_MODEL), lambda i, j: (0, 0)),
            pl.BlockSpec((D_MODEL, tn), lambda i, j: (0, j)),
            pl.BlockSpec((D_MODEL, V7X_LANES), lambda i, j: (0, 0)),
            pl.BlockSpec((1, tn), lambda i, j: (0, j)),
        ],
        out_specs=[
            pl.BlockSpec((tm, tn), lambda i, j: (i, j)),
            pl.BlockSpec((tm, V7X_LANES), lambda i, j: (i, 0)),
            pl.BlockSpec((tm // ATT_T, DIFF_V, ATT_T), lambda i, j: (i, 0, 0)),
        ],
        out_shape=[
            jax.ShapeDtypeStruct((S, PROJ_COLS), BF16),
            jax.ShapeDtypeStruct((S, V7X_LANES), F32),
            jax.ShapeDtypeStruct((S // ATT_T, DIFF_V, ATT_T), BF16),
        ],
        scratch_shapes=[pltpu.VMEM((tm, D_MODEL), BF16)],
        compiler_params=_params(("parallel", "arbitrary")),
        name="in_proj",
    )(x, g, w_main, w_ga, colscale)


def _log_sigmoid(x):
    return jnp.minimum(x, 0.0) - jnp.log(1.0 + jnp.exp(-jnp.abs(x)))


def _gla_kernel(q_ref, k_ref, v_ref, gate_ref, ga_ref, gan_ref, wup_ref, ba_ref, gn_ref,
                o_ref, st_scr, b_scr, d_scr, bn_scr, dn_scr, span_scr, a_scr):
    C, R = GLA_CHUNK, GLA_ROWS
    n_chunks = R // C

    row = lax.broadcasted_iota(jnp.int32, (R, R), 0)
    col = lax.broadcasted_iota(jnp.int32, (R, R), 1)
    shift = C.bit_length() - 1
    in_chunk = (col <= row) & ((row >> shift) == (col >> shift))
    tril = jnp.where(in_chunk, 1.0, 0.0).astype(BF16)
    gn = gn_ref[...]

    def decay_prefix(ga_blk):
        z = jnp.dot(ga_blk.astype(BF16), wup_ref[...], preferred_element_type=F32) + ba_ref[...]
        la = _log_sigmoid(z) * (1.0 / GLA_TAU)
        la_hi = la.astype(BF16)
        la_lo = (la - la_hi.astype(F32)).astype(BF16)
        b_all = (jnp.dot(tril, la_hi, preferred_element_type=F32)
                 + jnp.dot(tril, la_lo, preferred_element_type=F32))
        b_first = jnp.concatenate(
            [jnp.broadcast_to(b_all[c * C:c * C + 1], (C, GLA_QK)) for c in range(n_chunks)],
            axis=0)
        d = b_first - b_all
        bn_scr[...] = b_all
        dn_scr[...] = d
        span_scr[0] = jnp.max(d)

    @pl.when(pl.program_id(0) == 0)
    def _():
        st_scr[...] = jnp.zeros_like(st_scr)
        decay_prefix(ga_ref[...])

    b_scr[...] = bn_scr[...]
    d_scr[...] = dn_scr[...]
    small_span = span_scr[0] <= GLA_SAFE_SPAN

    def heads():
        for h in range(GLA_HEADS):
            ks = slice(h * GLA_DK, (h + 1) * GLA_DK)
            yield h, ks, q_ref[:, ks].astype(F32), k_ref[:, ks].astype(F32)

    @pl.when(small_span)
    def _():
        for h, ks, q, k in heads():
            d = d_scr[:, ks]
            q_t = (q * jnp.exp(-d)).astype(BF16)
            k_t = (k * jnp.exp(d)).astype(BF16)
            s = lax.dot_general(q_t, k_t, (((1,), (1,)), ((), ())),
                                preferred_element_type=F32)
            a_scr[h] = jnp.where(in_chunk, s, 0.0)

    @pl.when(jnp.logical_not(small_span))
    def _():
        a_scr[...] = jnp.zeros_like(a_scr)
        sub_row = lax.broadcasted_iota(jnp.int32, (GLA_SUB, C), 0)
        sub_col = lax.broadcasted_iota(jnp.int32, (GLA_SUB, C), 1)
        for h, ks, q_all, k_all in heads():
            for c in range(n_chunks):
                q = q_all[c * C:(c + 1) * C]
                k = k_all[c * C:(c + 1) * C]
                b = b_scr[c * C:(c + 1) * C, ks]
                for blk in range(C // GLA_SUB):
                    r0 = blk * GLA_SUB
                    b_blk = b[r0:r0 + GLA_SUB]
                    q_blk = q[r0:r0 + GLA_SUB]
                    b_ref0 = b[r0:r0 + 1]
                    a_blk = jnp.zeros((GLA_SUB, C), F32)
                    if blk > 0:
                        q_t = (q_blk * jnp.exp(b_blk - b_ref0)).astype(BF16)
                        k_t = (k * jnp.exp(jnp.minimum(b_ref0 - b, 0.0))).astype(BF16)
                        off = lax.dot_general(q_t, k_t, (((1,), (1,)), ((), ())),
                                              preferred_element_type=F32)
                        a_blk = jnp.where(sub_col < r0, off, 0.0)
                    for jj in range(GLA_SUB):
                        kj = k[r0 + jj:r0 + jj + 1]
                        bj = b[r0 + jj:r0 + jj + 1]
                        t = q_blk * kj * jnp.exp(jnp.minimum(b_blk - bj, 0.0))
                        cj = jnp.sum(t, axis=-1, keepdims=True)
                        a_blk = jnp.where((sub_col == r0 + jj) & (sub_row >= jj), cj, a_blk)
                    a_scr[h, c * C + r0:c * C + r0 + GLA_SUB, c * C:(c + 1) * C] = a_blk

    for h, ks, q, k in heads():
        vs = slice(h * GLA_DV, (h + 1) * GLA_DV)
        v = v_ref[:, vs]
        b = b_scr[:, ks]
        q_in = (q * jnp.exp(b)).astype(BF16)
        st = st_scr[h]
        o_inter = []
        for c in range(n_chunks):
            cr = slice(c * C, (c + 1) * C)
            o_inter.append(lax.dot_general(q_in[cr], st.astype(BF16), (((1,), (1,)), ((), ())),
                                           preferred_element_type=F32))
            b_last = b[(c + 1) * C - 1:(c + 1) * C]
            k_out = (k[cr] * jnp.exp(b_last - b[cr])).astype(BF16)
            upd = lax.dot_general(v[cr], k_out, (((0,), (0,)), ((), ())),
                                  preferred_element_type=F32)
            st = st * jnp.exp(b_last) + upd
        st_scr[h] = st
        o = (jnp.concatenate(o_inter, axis=0)
             + jnp.dot(a_scr[h].astype(BF16), v, preferred_element_type=F32))

        gate = gate_ref[:, vs].astype(F32)
        y = _rms(o, gn) * (gate * jax.nn.sigmoid(gate))
        o_ref[:, vs] = y.astype(BF16)

    decay_prefix(gan_ref[...])


def _gla(proj, ga, wup, b_alpha, gla_norm):
    S = proj.shape[0]
    R = GLA_ROWS
    return pl.pallas_call(
        _gla_kernel,
        grid=(S // R,),
        in_specs=[
            pl.BlockSpec((R, GLA_QK), lambda i: (i, COL_GQ // GLA_QK)),
            pl.BlockSpec((R, GLA_QK), lambda i: (i, COL_GK // GLA_QK)),
            pl.BlockSpec((R, GLA_V), lambda i: (i, COL_GV // GLA_V)),
            pl.BlockSpec((R, GLA_V), lambda i: (i, COL_GG // GLA_V)),
            pl.BlockSpec((R, V7X_LANES), lambda i: (i, 0)),
            pl.BlockSpec((R, V7X_LANES), lambda i: (jnp.minimum(i + 1, S // R - 1), 0)),
            pl.BlockSpec((V7X_LANES, GLA_QK), lambda i: (0, 0)),
            pl.BlockSpec((1, GLA_QK), lambda i: (0, 0)),
            pl.BlockSpec((1, GLA_DV), lambda i: (0, 0)),
        ],
        out_specs=pl.BlockSpec((R, GLA_V), lambda i: (i, 0)),
        out_shape=jax.ShapeDtypeStruct((S, GLA_V), BF16),
        scratch_shapes=[
            pltpu.VMEM((GLA_HEADS, GLA_DV, GLA_DK), F32),
            pltpu.VMEM((R, GLA_QK), F32),
            pltpu.VMEM((R, GLA_QK), F32),
            pltpu.VMEM((R, GLA_QK), F32),
            pltpu.VMEM((R, GLA_QK), F32),
            pltpu.SMEM((1,), F32),
            pltpu.VMEM((GLA_HEADS, R, R), F32),
        ],
        compiler_params=_params(("arbitrary",)),
        name="gla",
    )(proj, proj, proj, proj, ga, ga, wup, b_alpha, gla_norm)


def _attn_kernel(slopes_ref, lq1_ref, lk1_ref, lq2_ref, lk2_ref, gn_ref,
                 q_ref, k_ref, vt_ref, o_ref, feat_scr, kn_scr, run_scr, qa_scr, s_scr, m_scr,
                 l_scr, acc_scr):
    TQ, TK, QW, D = q_ref.shape[0], ATT_T, ATT_QW, DIFF_DQK
    h = pl.program_id(0)
    i = pl.program_id(1)
    n_kb = TQ // TK
    lane = lax.broadcasted_iota(jnp.int32, (TK, V7X_LANES), 1)

    slope2 = slopes_ref[h] * LOG2E
    blk_lane = lax.broadcasted_iota(jnp.int32, (1, V7X_LANES), 1)

    n_blocks = feat_scr.shape[0] // TK

    def fill_block(blk):
        r0 = pl.multiple_of(blk * TK, TK)
        row = lax.broadcasted_iota(jnp.int32, (TK, V7X_LANES), 0)
        b = slope2 * (row + r0).astype(F32)
        hi = b.astype(BF16).astype(F32)
        r1 = b - hi
        mid = r1.astype(BF16).astype(F32)
        lo = r1 - mid
        feat = jnp.where(lane == 0, hi,
                         jnp.where(lane == 1, mid, jnp.where(lane == 2, lo, 0.0)))
        feat_scr[pl.ds(r0, TK), :] = feat.astype(BF16)
        for c in range(2):
            kc = k_ref[pl.ds(r0, TK), c * D:(c + 1) * D].astype(F32)
            sq = jnp.sum(kc * kc, axis=-1, keepdims=True)
            run = jnp.maximum(run_scr[c], jnp.max(sq, axis=0, keepdims=True))
            run_scr[c] = run
            kn_scr[c] = jnp.where(blk_lane == blk, jnp.sqrt(run), kn_scr[c])

    @pl.when(i == 0)
    def _():
        run_scr[...] = jnp.zeros_like(run_scr)
        kn_scr[...] = jnp.zeros_like(kn_scr)
        fill_block(0)
        fill_block(1)

    q_lane = lax.broadcasted_iota(jnp.int32, (TQ, V7X_LANES), 1)
    q_ones = jnp.where(q_lane < 3, 1.0, 0.0).astype(BF16)
    for c in range(2):
        qa_scr[c, :, 0:D] = q_ref[:, c * D:(c + 1) * D]
        qa_scr[c, :, D:2 * D] = q_ones
    key_row = lax.broadcasted_iota(jnp.int32, (TK, QW), 0)
    query_col = lax.broadcasted_iota(jnp.int32, (TK, QW), 1)

    def slabs(key_off):
        return [sl for sl in range(TQ // QW)
                if key_off is None or key_off <= sl * QW + QW - 1]

    def scores(j, slot, key_off):
        r0 = pl.multiple_of(j * TK, TK)
        feat = feat_scr[pl.ds(r0, TK), :]
        for c in range(2):
            k = jnp.concatenate([k_ref[pl.ds(r0, TK), c * D:(c + 1) * D], feat], axis=1)
            for sl in slabs(key_off):
                qs = slice(sl * QW, (sl + 1) * QW)
                s_scr[slot, c, :, qs] = lax.dot_general(
                    k, qa_scr[c, qs, :], (((1,), (1,)), ((), ())),
                    preferred_element_type=F32)

    def accumulate(j, slot, key_off, first_for=()):
        vt = vt_ref[j]
        for c in range(2):
            for sl in slabs(key_off):
                q_lo = sl * QW
                qs = slice(q_lo, q_lo + QW)
                s = s_scr[slot, c, :, qs]
                if key_off is not None and key_off + TK - 1 > q_lo:
                    s = jnp.where(key_row + key_off <= query_col + q_lo, s, NEG_BIG)
                if sl in first_for:
                    m_new = jnp.max(s, axis=0, keepdims=True)
                    p = jnp.exp2(s - m_new)
                    l_scr[c, :, qs] = jnp.sum(p, axis=0, keepdims=True)
                    acc_scr[c, :, qs] = jnp.dot(vt, p.astype(BF16), preferred_element_type=F32)
                else:
                    m_prev = m_scr[c, :, qs]
                    m_new = jnp.maximum(m_prev, jnp.max(s, axis=0, keepdims=True))
                    alpha = jnp.exp2(m_prev - m_new)
                    p = jnp.exp2(s - m_new)
                    l_scr[c, :, qs] = (alpha * l_scr[c, :, qs]
                                       + jnp.sum(p, axis=0, keepdims=True))
                    acc_scr[c, :, qs] = alpha * acc_scr[c, :, qs] + jnp.dot(
                        vt, p.astype(BF16), preferred_element_type=F32)
                m_scr[c, :, qs] = m_new

    assert n_kb == 2
    scores(2 * i, 0, 0)
    scores(2 * i + 1, 1, TK)
    nxt = jnp.minimum(2 * i + 2, n_blocks - 2)
    fill_block(nxt)
    fill_block(nxt + 1)
    upper = tuple(range(TK // QW, TQ // QW))
    accumulate(2 * i + 1, 1, TK, first_for=upper)
    scores(jnp.maximum(2 * i - 1, 0), 1, None)
    accumulate(2 * i, 0, 0, first_for=tuple(range(TK // QW)))

    last_pos = (blk_lane * TK + (TK - 1)).astype(F32)
    skip = blk_lane < 2 * i
    for c in range(2):
        qf = q_ref[:, c * D:(c + 1) * D].astype(F32)
        qn = jnp.sqrt(jnp.max(jnp.sum(qf * qf, axis=-1, keepdims=True), axis=0, keepdims=True))
        bound = qn * kn_scr[c] * ATT_NORM_SLACK + slope2 * last_pos
        m_min = jnp.min(m_scr[c], axis=-1, keepdims=True)
        skip = skip & (bound < m_min - ATT_SKIP_LOG2)
    first_kept = jnp.min(jnp.where(skip, float(V7X_LANES), blk_lane.astype(F32)))
    n_skip = jnp.minimum(first_kept.astype(jnp.int32), 2 * i)
    n_pairs = i - n_skip // 2

    def pair(lo, prefetch):
        scores(lo, 0, None)
        accumulate(lo + 1, 1, None)
        if prefetch:
            scores(lo - 1, 1, None)
        accumulate(lo, 0, None)

    def body(t, carry):
        pair(2 * (i - 1 - t), True)
        return carry

    lax.fori_loop(0, n_pairs - 1, body, 0)

    @pl.when(n_pairs > 0)
    def _():
        pair(2 * (i - n_pairs), False)

    lam = (jnp.exp(jnp.sum(lq1_ref[...] * lk1_ref[...], axis=-1, keepdims=True))
           - jnp.exp(jnp.sum(lq2_ref[...] * lk2_ref[...], axis=-1, keepdims=True))
           + LAMBDA_INIT)
    o1 = acc_scr[0] / l_scr[0]
    o2 = acc_scr[1] / l_scr[1]
    o = (o1 - lam * o2).T
    o_ref[...] = (_rms(o, gn_ref[...]) * (1.0 - LAMBDA_INIT)).astype(BF16)


def _diff_attn(proj, vt, slopes, lq1, lk1, lq2, lk2, diff_norm):
    S = proj.shape[0]
    TQ = min(ATT_TQ, S)
    W = 2 * DIFF_DQK
    vec = pl.BlockSpec((1, DIFF_DQK), lambda h, i: (0, 0))
    return pl.pallas_call(
        _attn_kernel,
        grid=(DIFF_HEADS, S // TQ),
        in_specs=[
            pl.BlockSpec(memory_space=pltpu.SMEM),
            vec, vec, vec, vec,
            pl.BlockSpec((1, DIFF_DV), lambda h, i: (0, 0)),
            pl.BlockSpec((TQ, W), lambda h, i: (i, COL_DQ // W + h)),
            pl.BlockSpec((S, W), lambda h, i: (0, COL_DK // W + h)),
            pl.BlockSpec((S // ATT_T, DIFF_DV, ATT_T), lambda h, i: (0, h, 0)),
        ],
        out_specs=pl.BlockSpec((TQ, DIFF_DV), lambda h, i: (i, h)),
        out_shape=jax.ShapeDtypeStruct((S, DIFF_V), BF16),
        scratch_shapes=[
            pltpu.VMEM((S, V7X_LANES), BF16),
            pltpu.VMEM((2, 1, V7X_LANES), F32),
            pltpu.VMEM((2, 1, V7X_LANES), F32),
            pltpu.VMEM((2, TQ, W), BF16),
            pltpu.VMEM((2, 2, ATT_T, TQ), F32),
            pltpu.VMEM((2, 1, TQ), F32),
            pltpu.VMEM((2, 1, TQ), F32),
            pltpu.VMEM((2, DIFF_DV, TQ), F32),
        ],
        compiler_params=_params(("arbitrary", "arbitrary")),
        name="diff_attn",
    )(slopes, lq1, lk1, lq2, lk2, diff_norm, proj, proj, vt)


def _out_proj_kernel(oa_ref, ob_ref, wo_ref, x_ref, gpost_ref, gpre_ref,
                     x1_ref, h2_ref):
    for r0 in range(0, x_ref.shape[0], OUT_RH):
        rs = slice(r0, r0 + OUT_RH)
        m = (jnp.dot(oa_ref[rs, :], wo_ref[0:GLA_V, :], preferred_element_type=F32)
             + jnp.dot(ob_ref[rs, :], wo_ref[GLA_V:GLA_V + DIFF_V, :],
                       preferred_element_type=F32))
        x1 = x_ref[rs, :] + _rms(m, gpost_ref[...])
        x1_ref[rs, :] = x1
        h2_ref[rs, :] = _rms(x1, gpre_ref[...]).astype(BF16)


def _out_proj(o_a, o_b, w_o, x, g_post, g_pre):
    S = x.shape[0]
    tm = min(OUT_TM, S)
    row = lambda i: (i, 0)
    fixed = lambda i: (0, 0)
    return pl.pallas_call(
        _out_proj_kernel,
        grid=(S // tm,),
        in_specs=[
            pl.BlockSpec((tm, GLA_V), row),
            pl.BlockSpec((tm, DIFF_V), row),
            pl.BlockSpec((GLA_V + DIFF_V, D_MODEL), fixed),
            pl.BlockSpec((tm, D_MODEL), row),
            pl.BlockSpec((1, D_MODEL), fixed),
            pl.BlockSpec((1, D_MODEL), fixed),
        ],
        out_specs=[
            pl.BlockSpec((tm, D_MODEL), row),
            pl.BlockSpec((tm, D_MODEL), row),
        ],
        out_shape=[
            jax.ShapeDtypeStruct((S, D_MODEL), F32),
            jax.ShapeDtypeStruct((S, D_MODEL), BF16),
        ],
        compiler_params=_params(("parallel",)),
        name="out_proj",
    )(o_a, o_b, w_o, x, g_post, g_pre)


def _gelu_tanh(x):
    k = -2.0 * math.sqrt(2.0 / math.pi) * LOG2E
    return x / (1.0 + jnp.exp2(x * (k + (k * 0.044715) * (x * x))))


def _ffn_kernel(h2_ref, wa_ref, wb_ref, cw_ref, cb_ref, wout_ref, x1_ref, gpost_ref,
                o_ref, acc_scr, abuf_scr, halo_scr, *, tm):
    i = pl.program_id(0)
    j = pl.program_id(1)
    nj = pl.num_programs(1)
    P = V7X_SUBLANES

    @pl.when((i == 0) & (j == 0))
    def _():
        acc_scr[...] = jnp.zeros_like(acc_scr)
        halo_scr[...] = jnp.zeros_like(halo_scr)

    abuf_scr[0:P, :] = halo_scr[j]
    h2 = h2_ref[...]
    a = jnp.dot(h2, wa_ref[...], preferred_element_type=F32)
    b = jnp.dot(h2, wb_ref[...], preferred_element_type=F32)
    abuf_scr[P:P + tm, :] = a
    a1 = abuf_scr[P - 1:P - 1 + tm, :]
    a2 = abuf_scr[P - 2:P - 2 + tm, :]
    conv = cw_ref[0:1, :] * a2 + cw_ref[1:2, :] * a1 + cw_ref[2:3, :] * a + cb_ref[...]
    g = (_gelu_tanh(conv) * b).astype(BF16)
    acc_scr[...] += jnp.dot(g, wout_ref[...], preferred_element_type=F32)
    halo_scr[j] = abuf_scr[tm:tm + P, :]

    @pl.when(j == nj - 1)
    def _():
        o_ref[...] = x1_ref[...] + _rms(acc_scr[...], gpost_ref[...])
        acc_scr[...] = jnp.zeros_like(acc_scr)


def _ffn(h2, w_ffn_in, conv_w, conv_b, w_ffn_out, x1, g_post):
    S = h2.shape[0]
    tm = min(FFN_TM, S)
    tn = FFN_TN
    nj = D_FF // tn
    row = lambda i, j: (i, 0)
    return pl.pallas_call(
        functools.partial(_ffn_kernel, tm=tm),
        grid=(S // tm, nj),
        in_specs=[
            pl.BlockSpec((tm, D_MODEL), row),
            pl.BlockSpec((D_MODEL, tn), lambda i, j: (0, j)),
            pl.BlockSpec((D_MODEL, tn), lambda i, j: (0, nj + j)),
            pl.BlockSpec((CONV_W, tn), lambda i, j: (0, j)),
            pl.BlockSpec((1, tn), lambda i, j: (0, j)),
            pl.BlockSpec((tn, D_MODEL), lambda i, j: (j, 0)),
            pl.BlockSpec((tm, D_MODEL), row),
            pl.BlockSpec((1, D_MODEL), lambda i, j: (0, 0)),
        ],
        out_specs=pl.BlockSpec((tm, D_MODEL), row),
        out_shape=jax.ShapeDtypeStruct((S, D_MODEL), F32),
        scratch_shapes=[
            pltpu.VMEM((tm, D_MODEL), F32),
            pltpu.VMEM((tm + V7X_SUBLANES, tn), F32),
            pltpu.VMEM((nj, V7X_SUBLANES, tn), F32),
        ],
        compiler_params=_params(("arbitrary", "arbitrary")),
        name="ffn",
    )(h2, w_ffn_in, w_ffn_in, conv_w, conv_b, w_ffn_out, x1, g_post)


def _layer(x, attn_pre_norm, w_in, w_alpha_up, b_alpha, gla_norm, lambda_q1, lambda_k1,
           lambda_q2, lambda_k2, diff_norm, w_o, attn_post_norm, ffn_pre_norm, w_ffn_in,
           conv_w, conv_b, w_ffn_out, ffn_post_norm):
    vec = lambda p: p.reshape(1, -1).astype(F32)

    w_in16 = w_in.astype(BF16)
    w_main = jnp.concatenate(
        [w_in16[:, :GA_OFFSET], w_in16[:, GA_OFFSET + GLA_RANK:]], axis=1)
    w_ga = jnp.pad(w_in16[:, GA_OFFSET:GA_OFFSET + GLA_RANK],
                   ((0, 0), (0, V7X_LANES - GLA_RANK)))
    colscale = jnp.concatenate([
        jnp.full((GLA_QK,), GLA_DK ** -0.5, F32),
        jnp.ones((COL_DQ - COL_GK,), F32),
        jnp.full((DIFF_QK,), DIFF_DQK ** -0.5 * LOG2E, F32),
        jnp.ones((PROJ_COLS - COL_DK,), F32),
    ]).reshape(1, PROJ_COLS)
    wup = jnp.pad(w_alpha_up, ((0, V7X_LANES - GLA_RANK), (0, 0))).astype(BF16)
    slopes = jnp.asarray(
        [2.0 ** (-8.0 * (h + 1) / DIFF_HEADS) for h in range(DIFF_HEADS)], F32)

    proj, ga, vt = _in_proj(x, vec(attn_pre_norm), w_main, w_ga, colscale)
    o_a = _gla(proj, ga, wup, vec(b_alpha), vec(gla_norm))
    o_b = _diff_attn(proj, vt, slopes, vec(lambda_q1), vec(lambda_k1), vec(lambda_q2),
                     vec(lambda_k2), vec(diff_norm))
    x1, h2 = _out_proj(o_a, o_b, w_o.astype(BF16), x, vec(attn_post_norm),
                       vec(ffn_pre_norm))
    return _ffn(h2, w_ffn_in.astype(BF16), conv_w.astype(F32), vec(conv_b),
                w_ffn_out.astype(BF16), x1, vec(ffn_post_norm))


def kernel(x, attn_pre_norm, w_in, w_alpha_up, b_alpha, gla_norm, lambda_q1, lambda_k1,
           lambda_q2, lambda_k2, diff_norm, w_o, attn_post_norm, ffn_pre_norm, w_ffn_in,
           conv_w, conv_b, w_ffn_out, ffn_post_norm):
    B = x.shape[0]
    depth = w_in.shape[0]
    assert depth == 1, "lambda_init is baked for a single layer"
    outs = []
    for bi in range(B):
        xb = x[bi]
        for l in range(depth):
            xb = _layer(xb, attn_pre_norm[l], w_in[l], w_alpha_up[l], b_alpha[l], gla_norm[l],
                        lambda_q1[l], lambda_k1[l], lambda_q2[l], lambda_k2[l], diff_norm[l],
                        w_o[l], attn_post_norm[l], ffn_pre_norm[l], w_ffn_in[l], conv_w[l],
                        conv_b[l], w_ffn_out[l], ffn_post_norm[l])
        outs.append(xb)
    return outs[0][None] if B == 1 else jnp.stack(outs, axis=0)
```

```python
import functools
import math

import jax
import jax.numpy as jnp
from jax import lax
from jax.experimental import pallas as pl
from jax.experimental.pallas import tpu as pltpu

F32 = jnp.float32
BF16 = jnp.bfloat16

D_MODEL = 2048
GLA_HEADS = 4
GLA_DK = 128
GLA_DV = 256
GLA_RANK = 16
GLA_TAU = 16.0
DIFF_HEADS = 4
DIFF_DQK = 128
DIFF_DV = 256
D_FF = 5632
CONV_W = 3
EPS = 1e-6
LAMBDA_INIT = 0.8 - 0.6 * math.exp(-0.3 * 0)
LOG2E = math.log2(math.e)

GLA_QK = GLA_HEADS * GLA_DK
GLA_V = GLA_HEADS * GLA_DV
DIFF_QK = DIFF_HEADS * 2 * DIFF_DQK
DIFF_V = DIFF_HEADS * DIFF_DV

COL_GQ = 0
COL_GK = COL_GQ + GLA_QK
COL_GV = COL_GK + GLA_QK
COL_GG = COL_GV + GLA_V
COL_DQ = COL_GG + GLA_V
COL_DK = COL_DQ + DIFF_QK
COL_DV = COL_DK + DIFF_QK
PROJ_COLS = COL_DV + DIFF_V
GA_OFFSET = GLA_QK + GLA_QK + GLA_V + GLA_V

V7X_LANES = 128
V7X_SUBLANES = 8
V7X_VMEM_LIMIT_BYTES = 56 * 1024 * 1024

IN_TM = 1024
IN_TN = 1536
OUT_TM = 512
OUT_RH = 256
FFN_TM = 512
FFN_TN = 512
ATT_T = 512
ATT_TQ = 1024
ATT_QW = 256
ATT_SKIP_LOG2 = 160.0
ATT_NORM_SLACK = 1.001
GLA_CHUNK = 64
GLA_ROWS = 256
GLA_SUB = 16
GLA_SAFE_SPAN = 60.0
NEG_BIG = -1e30


def _rms(v, g):
    return v * lax.rsqrt(jnp.mean(v * v, axis=-1, keepdims=True) + EPS) * g


def _params(dims):
    return pltpu.CompilerParams(dimension_semantics=dims,
                                vmem_limit_bytes=V7X_VMEM_LIMIT_BYTES)


def _in_proj_kernel(x_ref, g_ref, w_ref, wga_ref, cs_ref, o_ref, ga_ref, vt_ref, h_scr,
                    *, tn):
    j = pl.program_id(1)

    @pl.when(j == 0)
    def _():
        h = _rms(x_ref[...], g_ref[...]).astype(BF16)
        h_scr[...] = h
        ga_ref[...] = jnp.dot(h, wga_ref[...], preferred_element_type=F32)

    acc = jnp.dot(h_scr[...], w_ref[...], preferred_element_type=F32)
    o_ref[...] = (acc * cs_ref[...]).astype(BF16)

    @pl.when(j == COL_DV // tn)
    def _():
        v0 = COL_DV % tn
        acc_t = acc[:, v0:v0 + DIFF_V].T.astype(BF16)
        for kb in range(vt_ref.shape[0]):
            vt_ref[kb] = acc_t[:, kb * ATT_T:(kb + 1) * ATT_T]


def _in_proj(x, g, w_main, w_ga, colscale):
    S = x.shape[0]
    tm = min(IN_TM, S)
    tn = IN_TN
    return pl.pallas_call(
        functools.partial(_in_proj_kernel, tn=tn),
        grid=(S // tm, PROJ_COLS // tn),
        in_specs=[
            pl.BlockSpec((tm, D_MODEL), lambda i, j: (i, 0)),
            pl.BlockSpec((1, D_MODEL), lambda i, j: (0, 0)),
            pl.BlockSpec((D_MODEL, tn), lambda i, j: (0, j)),
            pl.BlockSpec((D_MODEL, V7X_LANES), lambda i, j: (0, 0)),
            pl.BlockSpec((1, tn), lambda i, j: (0, j)),
        ],
        out_specs=[
            pl.BlockSpec((tm, tn), lambda i, j: (i, j)),
            pl.BlockSpec((tm, V7X_LANES), lambda i, j: (i, 0)),
            pl.BlockSpec((tm // ATT_T, DIFF_V, ATT_T), lambda i, j: (i, 0, 0)),
        ],
        out_shape=[
            jax.ShapeDtypeStruct((S, PROJ_COLS), BF16),
            jax.ShapeDtypeStruct((S, V7X_LANES), F32),
            jax.ShapeDtypeStruct((S // ATT_T, DIFF_V, ATT_T), BF16),
        ],
        scratch_shapes=[pltpu.VMEM((tm, D_MODEL), BF16)],
        compiler_params=_params(("parallel", "arbitrary")),
        name="in_proj",
    )(x, g, w_main, w_ga, colscale)


def _log_sigmoid(x):
    return jnp.minimum(x, 0.0) - jnp.log(1.0 + jnp.exp(-jnp.abs(x)))


def _gla_kernel(q_ref, k_ref, v_ref, gate_ref, ga_ref, gan_ref, wup_ref, ba_ref, gn_ref,
                o_ref, st_scr, b_scr, d_scr, bn_scr, dn_scr, span_scr, a_scr):
    C, R = GLA_CHUNK, GLA_ROWS
    n_chunks = R // C

    row = lax.broadcasted_iota(jnp.int32, (R, R), 0)
    col = lax.broadcasted_iota(jnp.int32, (R, R), 1)
    shift = C.bit_length() - 1
    in_chunk = (col <= row) & ((row >> shift) == (col >> shift))
    tril = jnp.where(in_chunk, 1.0, 0.0).astype(BF16)
    gn = gn_ref[...]

    def decay_prefix(ga_blk):
        z = jnp.dot(ga_blk.astype(BF16), wup_ref[...], preferred_element_type=F32) + ba_ref[...]
        la = _log_sigmoid(z) * (1.0 / GLA_TAU)
        la_hi = la.astype(BF16)
        la_lo = (la - la_hi.astype(F32)).astype(BF16)
        b_all = (jnp.dot(tril, la_hi, preferred_element_type=F32)
                 + jnp.dot(tril, la_lo, preferred_element_type=F32))
        b_first = jnp.concatenate(
            [jnp.broadcast_to(b_all[c * C:c * C + 1], (C, GLA_QK)) for c in range(n_chunks)],
            axis=0)
        d = b_first - b_all
        bn_scr[...] = b_all
        dn_scr[...] = d
        span_scr[0] = jnp.max(d)

    @pl.when(pl.program_id(0) == 0)
    def _():
        st_scr[...] = jnp.zeros_like(st_scr)
        decay_prefix(ga_ref[...])

    b_scr[...] = bn_scr[...]
    d_scr[...] = dn_scr[...]
    small_span = span_scr[0] <= GLA_SAFE_SPAN

    def heads():
        for h in range(GLA_HEADS):
            ks = slice(h * GLA_DK, (h + 1) * GLA_DK)
            yield h, ks, q_ref[:, ks].astype(F32), k_ref[:, ks].astype(F32)

    @pl.when(small_span)
    def _():
        for h, ks, q, k in heads():
            d = d_scr[:, ks]
            q_t = (q * jnp.exp(-d)).astype(BF16)
            k_t = (k * jnp.exp(d)).astype(BF16)
            s = lax.dot_general(q_t, k_t, (((1,), (1,)), ((), ())),
                                preferred_element_type=F32)
            a_scr[h] = jnp.where(in_chunk, s, 0.0)

    @pl.when(jnp.logical_not(small_span))
    def _():
        a_scr[...] = jnp.zeros_like(a_scr)
        sub_row = lax.broadcasted_iota(jnp.int32, (GLA_SUB, C), 0)
        sub_col = lax.broadcasted_iota(jnp.int32, (GLA_SUB, C), 1)
        for h, ks, q_all, k_all in heads():
            for c in range(n_chunks):
                q = q_all[c * C:(c + 1) * C]
                k = k_all[c * C:(c + 1) * C]
                b = b_scr[c * C:(c + 1) * C, ks]
                for blk in range(C // GLA_SUB):
                    r0 = blk * GLA_SUB
                    b_blk = b[r0:r0 + GLA_SUB]
                    q_blk = q[r0:r0 + GLA_SUB]
                    b_ref0 = b[r0:r0 + 1]
                    a_blk = jnp.zeros((GLA_SUB, C), F32)
                    if blk > 0:
                        q_t = (q_blk * jnp.exp(b_blk - b_ref0)).astype(BF16)
                        k_t = (k * jnp.exp(jnp.minimum(b_ref0 - b, 0.0))).astype(BF16)
                        off = lax.dot_general(q_t, k_t, (((1,), (1,)), ((), ())),
                                              preferred_element_type=F32)
                        a_blk = jnp.where(sub_col < r0, off, 0.0)
                    for jj in range(GLA_SUB):
                        kj = k[r0 + jj:r0 + jj + 1]
                        bj = b[r0 + jj:r0 + jj + 1]
                        t = q_blk * kj * jnp.exp(jnp.minimum(b_blk - bj, 0.0))
                        cj = jnp.sum(t, axis=-1, keepdims=True)
                        a_blk = jnp.where((sub_col == r0 + jj) & (sub_row >= jj), cj, a_blk)
                    a_scr[h, c * C + r0:c * C + r0 + GLA_SUB, c * C:(c + 1) * C] = a_blk

    for h, ks, q, k in heads():
        vs = slice(h * GLA_DV, (h + 1) * GLA_DV)
        v = v_ref[:, vs]
        b = b_scr[:, ks]
        q_in = (q * jnp.exp(b)).astype(BF16)
        st = st_scr[h]
        o_inter = []
        for c in range(n_chunks):
            cr = slice(c * C, (c + 1) * C)
            o_inter.append(lax.dot_general(q_in[cr], st.astype(BF16), (((1,), (1,)), ((), ())),
                                           preferred_element_type=F32))
            b_last = b[(c + 1) * C - 1:(c + 1) * C]
            k_out = (k[cr] * jnp.exp(b_last - b[cr])).astype(BF16)
            upd = lax.dot_general(v[cr], k_out, (((0,), (0,)), ((), ())),
                                  preferred_element_type=F32)
            st = st * jnp.exp(b_last) + upd
        st_scr[h] = st
        o = (jnp.concatenate(o_inter, axis=0)
             + jnp.dot(a_scr[h].astype(BF16), v, preferred_element_type=F32))

        gate = gate_ref[:, vs].astype(F32)
        y = _rms(o, gn) * (gate * jax.nn.sigmoid(gate))
        o_ref[:, vs] = y.astype(BF16)

    decay_prefix(gan_ref[...])


def _gla(proj, ga, wup, b_alpha, gla_norm):
    S = proj.shape[0]
    R = GLA_ROWS
    return pl.pallas_call(
        _gla_kernel,
        grid=(S // R,),
        in_specs=[
            pl.BlockSpec((R, GLA_QK), lambda i: (i, COL_GQ // GLA_QK)),
            pl.BlockSpec((R, GLA_QK), lambda i: (i, COL_GK // GLA_QK)),
            pl.BlockSpec((R, GLA_V), lambda i: (i, COL_GV // GLA_V)),
            pl.BlockSpec((R, GLA_V), lambda i: (i, COL_GG // GLA_V)),
            pl.BlockSpec((R, V7X_LANES), lambda i: (i, 0)),
            pl.BlockSpec((R, V7X_LANES), lambda i: (jnp.minimum(i + 1, S // R - 1), 0)),
            pl.BlockSpec((V7X_LANES, GLA_QK), lambda i: (0, 0)),
            pl.BlockSpec((1, GLA_QK), lambda i: (0, 0)),
            pl.BlockSpec((1, GLA_DV), lambda i: (0, 0)),
        ],
        out_specs=pl.BlockSpec((R, GLA_V), lambda i: (i, 0)),
        out_shape=jax.ShapeDtypeStruct((S, GLA_V), BF16),
        scratch_shapes=[
            pltpu.VMEM((GLA_HEADS, GLA_DV, GLA_DK), F32),
            pltpu.VMEM((R, GLA_QK), F32),
            pltpu.VMEM((R, GLA_QK), F32),
            pltpu.VMEM((R, GLA_QK), F32),
            pltpu.VMEM((R, GLA_QK), F32),
            pltpu.SMEM((1,), F32),
            pltpu.VMEM((GLA_HEADS, R, R), F32),
        ],
        compiler_params=_params(("arbitrary",)),
        name="gla",
    )(proj, proj, proj, proj, ga, ga, wup, b_alpha, gla_norm)


def _attn_kernel(slopes_ref, lq1_ref, lk1_ref, lq2_ref, lk2_ref, gn_ref,
                 q_ref, k_ref, vt_ref, o_ref, feat_scr, kn_scr, run_scr, qa_scr, s_scr, m_scr,
                 l_scr, acc_scr):
    TQ, TK, QW, D = q_ref.shape[0], ATT_T, ATT_QW, DIFF_DQK
    h = pl.program_id(0)
    i = pl.program_id(1)
    n_kb = TQ // TK
    lane = lax.broadcasted_iota(jnp.int32, (TK, V7X_LANES), 1)

    slope2 = slopes_ref[h] * LOG2E
    blk_lane = lax.broadcasted_iota(jnp.int32, (1, V7X_LANES), 1)

    n_blocks = feat_scr.shape[0] // TK

    def fill_block(blk):
        r0 = pl.multiple_of(blk * TK, TK)
        row = lax.broadcasted_iota(jnp.int32, (TK, V7X_LANES), 0)
        b = slope2 * (row + r0).astype(F32)
        hi = b.astype(BF16).astype(F32)
        r1 = b - hi
        mid = r1.astype(BF16).astype(F32)
        lo = r1 - mid
        feat = jnp.where(lane == 0, hi,
                         jnp.where(lane == 1, mid, jnp.where(lane == 2, lo, 0.0)))
        feat_scr[pl.ds(r0, TK), :] = feat.astype(BF16)
        for c in range(2):
            kc = k_ref[pl.ds(r0, TK), c * D:(c + 1) * D].astype(F32)
            sq = jnp.sum(kc * kc, axis=-1, keepdims=True)
            run = jnp.maximum(run_scr[c], jnp.max(sq, axis=0, keepdims=True))
            run_scr[c] = run
            kn_scr[c] = jnp.where(blk_lane == blk, jnp.sqrt(run), kn_scr[c])

    @pl.when(i == 0)
    def _():
        run_scr[...] = jnp.zeros_like(run_scr)
        kn_scr[...] = jnp.zeros_like(kn_scr)
        fill_block(0)
        fill_block(1)

    m_scr[...] = jnp.full_like(m_scr, NEG_BIG)
    l_scr[...] = jnp.zeros_like(l_scr)
    acc_scr[...] = jnp.zeros_like(acc_scr)

    q_lane = lax.broadcasted_iota(jnp.int32, (TQ, V7X_LANES), 1)
    q_ones = jnp.where(q_lane < 3, 1.0, 0.0).astype(BF16)
    for c in range(2):
        qa_scr[c, :, 0:D] = q_ref[:, c * D:(c + 1) * D]
        qa_scr[c, :, D:2 * D] = q_ones
    key_row = lax.broadcasted_iota(jnp.int32, (TK, QW), 0)
    query_col = lax.broadcasted_iota(jnp.int32, (TK, QW), 1)

    def slabs(key_off):
        return [sl for sl in range(TQ // QW)
                if key_off is None or key_off <= sl * QW + QW - 1]

    def scores(j, slot, key_off):
        r0 = pl.multiple_of(j * TK, TK)
        feat = feat_scr[pl.ds(r0, TK), :]
        for c in range(2):
            k = jnp.concatenate([k_ref[pl.ds(r0, TK), c * D:(c + 1) * D], feat], axis=1)
            for sl in slabs(key_off):
                qs = slice(sl * QW, (sl + 1) * QW)
                s_scr[slot, c, :, qs] = lax.dot_general(
                    k, qa_scr[c, qs, :], (((1,), (1,)), ((), ())),
                    preferred_element_type=F32)

    def accumulate(j, slot, key_off):
        vt = vt_ref[j]
        for c in range(2):
            for sl in slabs(key_off):
                q_lo = sl * QW
                qs = slice(q_lo, q_lo + QW)
                s = s_scr[slot, c, :, qs]
                if key_off is not None and key_off + TK - 1 > q_lo:
                    s = jnp.where(key_row + key_off <= query_col + q_lo, s, NEG_BIG)
                m_prev = m_scr[c, :, qs]
                m_new = jnp.maximum(m_prev, jnp.max(s, axis=0, keepdims=True))
                alpha = jnp.exp2(m_prev - m_new)
                p = jnp.exp2(s - m_new)
                l_scr[c, :, qs] = alpha * l_scr[c, :, qs] + jnp.sum(p, axis=0, keepdims=True)
                acc_scr[c, :, qs] = alpha * acc_scr[c, :, qs] + jnp.dot(
                    vt, p.astype(BF16), preferred_element_type=F32)
                m_scr[c, :, qs] = m_new

    assert n_kb == 2
    scores(2 * i, 0, 0)
    scores(2 * i + 1, 1, TK)
    nxt = jnp.minimum(2 * i + 2, n_blocks - 2)
    fill_block(nxt)
    fill_block(nxt + 1)
    accumulate(2 * i + 1, 1, TK)
    scores(jnp.maximum(2 * i - 1, 0), 1, None)
    accumulate(2 * i, 0, 0)

    last_pos = (blk_lane * TK + (TK - 1)).astype(F32)
    skip = blk_lane < 2 * i
    for c in range(2):
        qf = q_ref[:, c * D:(c + 1) * D].astype(F32)
        qn = jnp.sqrt(jnp.max(jnp.sum(qf * qf, axis=-1, keepdims=True), axis=0, keepdims=True))
        bound = qn * kn_scr[c] * ATT_NORM_SLACK + slope2 * last_pos
        m_min = jnp.min(m_scr[c], axis=-1, keepdims=True)
        skip = skip & (bound < m_min - ATT_SKIP_LOG2)
    first_kept = jnp.min(jnp.where(skip, float(V7X_LANES), blk_lane.astype(F32)))
    n_skip = jnp.minimum(first_kept.astype(jnp.int32), 2 * i)
    n_pairs = i - n_skip // 2

    def pair(lo, prefetch):
        scores(lo, 0, None)
        accumulate(lo + 1, 1, None)
        if prefetch:
            scores(lo - 1, 1, None)
        accumulate(lo, 0, None)

    def body(t, carry):
        pair(2 * (i - 1 - t), True)
        return carry

    lax.fori_loop(0, n_pairs - 1, body, 0)

    @pl.when(n_pairs > 0)
    def _():
        pair(2 * (i - n_pairs), False)

    lam = (jnp.exp(jnp.sum(lq1_ref[...] * lk1_ref[...], axis=-1, keepdims=True))
           - jnp.exp(jnp.sum(lq2_ref[...] * lk2_ref[...], axis=-1, keepdims=True))
           + LAMBDA_INIT)
    o1 = acc_scr[0] / l_scr[0]
    o2 = acc_scr[1] / l_scr[1]
    o = (o1 - lam * o2).T
    o_ref[...] = (_rms(o, gn_ref[...]) * (1.0 - LAMBDA_INIT)).astype(BF16)


def _diff_attn(proj, vt, slopes, lq1, lk1, lq2, lk2, diff_norm):
    S = proj.shape[0]
    TQ = min(ATT_TQ, S)
    W = 2 * DIFF_DQK
    vec = pl.BlockSpec((1, DIFF_DQK), lambda h, i: (0, 0))
    return pl.pallas_call(
        _attn_kernel,
        grid=(DIFF_HEADS, S // TQ),
        in_specs=[
            pl.BlockSpec(memory_space=pltpu.SMEM),
            vec, vec, vec, vec,
            pl.BlockSpec((1, DIFF_DV), lambda h, i: (0, 0)),
            pl.BlockSpec((TQ, W), lambda h, i: (i, COL_DQ // W + h)),
            pl.BlockSpec((S, W), lambda h, i: (0, COL_DK // W + h)),
            pl.BlockSpec((S // ATT_T, DIFF_DV, ATT_T), lambda h, i: (0, h, 0)),
        ],
        out_specs=pl.BlockSpec((TQ, DIFF_DV), lambda h, i: (i, h)),
        out_shape=jax.ShapeDtypeStruct((S, DIFF_V), BF16),
        scratch_shapes=[
            pltpu.VMEM((S, V7X_LANES), BF16),
            pltpu.VMEM((2, 1, V7X_LANES), F32),
            pltpu.VMEM((2, 1, V7X_LANES), F32),
            pltpu.VMEM((2, TQ, W), BF16),
            pltpu.VMEM((2, 2, ATT_T, TQ), F32),
            pltpu.VMEM((2, 1, TQ), F32),
            pltpu.VMEM((2, 1, TQ), F32),
            pltpu.VMEM((2, DIFF_DV, TQ), F32),
        ],
        compiler_params=_params(("arbitrary", "arbitrary")),
        name="diff_attn",
    )(slopes, lq1, lk1, lq2, lk2, diff_norm, proj, proj, vt)


def _out_proj_kernel(oa_ref, ob_ref, wo_ref, x_ref, gpost_ref, gpre_ref,
                     x1_ref, h2_ref):
    for r0 in range(0, x_ref.shape[0], OUT_RH):
        rs = slice(r0, r0 + OUT_RH)
        m = (jnp.dot(oa_ref[rs, :], wo_ref[0:GLA_V, :], preferred_element_type=F32)
             + jnp.dot(ob_ref[rs, :], wo_ref[GLA_V:GLA_V + DIFF_V, :],
                       preferred_element_type=F32))
        x1 = x_ref[rs, :] + _rms(m, gpost_ref[...])
        x1_ref[rs, :] = x1
        h2_ref[rs, :] = _rms(x1, gpre_ref[...]).astype(BF16)


def _out_proj(o_a, o_b, w_o, x, g_post, g_pre):
    S = x.shape[0]
    tm = min(OUT_TM, S)
    row = lambda i: (i, 0)
    fixed = lambda i: (0, 0)
    return pl.pallas_call(
        _out_proj_kernel,
        grid=(S // tm,),
        in_specs=[
            pl.BlockSpec((tm, GLA_V), row),
            pl.BlockSpec((tm, DIFF_V), row),
            pl.BlockSpec((GLA_V + DIFF_V, D_MODEL), fixed),
            pl.BlockSpec((tm, D_MODEL), row),
            pl.BlockSpec((1, D_MODEL), fixed),
            pl.BlockSpec((1, D_MODEL), fixed),
        ],
        out_specs=[
            pl.BlockSpec((tm, D_MODEL), row),
            pl.BlockSpec((tm, D_MODEL), row),
        ],
        out_shape=[
            jax.ShapeDtypeStruct((S, D_MODEL), F32),
            jax.ShapeDtypeStruct((S, D_MODEL), BF16),
        ],
        compiler_params=_params(("parallel",)),
        name="out_proj",
    )(o_a, o_b, w_o, x, g_post, g_pre)


def _gelu_tanh(x):
    k = -2.0 * math.sqrt(2.0 / math.pi) * LOG2E
    return x / (1.0 + jnp.exp2(x * (k + (k * 0.044715) * (x * x))))


def _ffn_kernel(h2_ref, wa_ref, wb_ref, cw_ref, cb_ref, wout_ref, x1_ref, gpost_ref,
                o_ref, acc_scr, abuf_scr, halo_scr, *, tm):
    i = pl.program_id(0)
    j = pl.program_id(1)
    nj = pl.num_programs(1)
    P = V7X_SUBLANES

    @pl.when((i == 0) & (j == 0))
    def _():
        acc_scr[...] = jnp.zeros_like(acc_scr)
        halo_scr[...] = jnp.zeros_like(halo_scr)

    abuf_scr[0:P, :] = halo_scr[j]
    h2 = h2_ref[...]
    a = jnp.dot(h2, wa_ref[...], preferred_element_type=F32)
    b = jnp.dot(h2, wb_ref[...], preferred_element_type=F32)
    abuf_scr[P:P + tm, :] = a
    a1 = abuf_scr[P - 1:P - 1 + tm, :]
    a2 = abuf_scr[P - 2:P - 2 + tm, :]
    conv = cw_ref[0:1, :] * a2 + cw_ref[1:2, :] * a1 + cw_ref[2:3, :] * a + cb_ref[...]
    g = (_gelu_tanh(conv) * b).astype(BF16)
    acc_scr[...] += jnp.dot(g, wout_ref[...], preferred_element_type=F32)
    halo_scr[j] = abuf_scr[tm:tm + P, :]

    @pl.when(j == nj - 1)
    def _():
        o_ref[...] = x1_ref[...] + _rms(acc_scr[...], gpost_ref[...])
        acc_scr[...] = jnp.zeros_like(acc_scr)


def _ffn(h2, w_ffn_in, conv_w, conv_b, w_ffn_out, x1, g_post):
    S = h2.shape[0]
    tm = min(FFN_TM, S)
    tn = FFN_TN
    nj = D_FF // tn
    row = lambda i, j: (i, 0)
    return pl.pallas_call(
        functools.partial(_ffn_kernel, tm=tm),
        grid=(S // tm, nj),
        in_specs=[
            pl.BlockSpec((tm, D_MODEL), row),
            pl.BlockSpec((D_MODEL, tn), lambda i, j: (0, j)),
            pl.BlockSpec((D_MODEL, tn), lambda i, j: (0, nj + j)),
            pl.BlockSpec((CONV_W, tn), lambda i, j: (0, j)),
            pl.BlockSpec((1, tn), lambda i, j: (0, j)),
            pl.BlockSpec((tn, D_MODEL), lambda i, j: (j, 0)),
            pl.BlockSpec((tm, D_MODEL), row),
            pl.BlockSpec((1, D_MODEL), lambda i, j: (0, 0)),
        ],
        out_specs=pl.BlockSpec((tm, D_MODEL), row),
        out_shape=jax.ShapeDtypeStruct((S, D_MODEL), F32),
        scratch_shapes=[
            pltpu.VMEM((tm, D_MODEL), F32),
            pltpu.VMEM((tm + V7X_SUBLANES, tn), F32),
            pltpu.VMEM((nj, V7X_SUBLANES, tn), F32),
        ],
        compiler_params=_params(("arbitrary", "arbitrary")),
        name="ffn",
    )(h2, w_ffn_in, w_ffn_in, conv_w, conv_b, w_ffn_out, x1, g_post)


def _layer(x, attn_pre_norm, w_in, w_alpha_up, b_alpha, gla_norm, lambda_q1, lambda_k1,
           lambda_q2, lambda_k2, diff_norm, w_o, attn_post_norm, ffn_pre_norm, w_ffn_in,
           conv_w, conv_b, w_ffn_out, ffn_post_norm):
    vec = lambda p: p.reshape(1, -1).astype(F32)

    w_in16 = w_in.astype(BF16)
    w_main = jnp.concatenate(
        [w_in16[:, :GA_OFFSET], w_in16[:, GA_OFFSET + GLA_RANK:]], axis=1)
    w_ga = jnp.pad(w_in16[:, GA_OFFSET:GA_OFFSET + GLA_RANK],
                   ((0, 0), (0, V7X_LANES - GLA_RANK)))
    colscale = jnp.concatenate([
        jnp.full((GLA_QK,), GLA_DK ** -0.5, F32),
        jnp.ones((COL_DQ - COL_GK,), F32),
        jnp.full((DIFF_QK,), DIFF_DQK ** -0.5 * LOG2E, F32),
        jnp.ones((PROJ_COLS - COL_DK,), F32),
    ]).reshape(1, PROJ_COLS)
    wup = jnp.pad(w_alpha_up, ((0, V7X_LANES - GLA_RANK), (0, 0))).astype(BF16)
    slopes = jnp.asarray(
        [2.0 ** (-8.0 * (h + 1) / DIFF_HEADS) for h in range(DIFF_HEADS)], F32)

    proj, ga, vt = _in_proj(x, vec(attn_pre_norm), w_main, w_ga, colscale)
    o_a = _gla(proj, ga, wup, vec(b_alpha), vec(gla_norm))
    o_b = _diff_attn(proj, vt, slopes, vec(lambda_q1), vec(lambda_k1), vec(lambda_q2),
                     vec(lambda_k2), vec(diff_norm))
    x1, h2 = _out_proj(o_a, o_b, w_o.astype(BF16), x, vec(attn_post_norm),
                       vec(ffn_pre_norm))
    return _ffn(h2, w_ffn_in.astype(BF16), conv_w.astype(F32), vec(conv_b),
                w_ffn_out.astype(BF16), x1, vec(ffn_post_norm))


def kernel(x, attn_pre_norm, w_in, w_alpha_up, b_alpha, gla_norm, lambda_q1, lambda_k1,
           lambda_q2, lambda_k2, diff_norm, w_o, attn_post_norm, ffn_pre_norm, w_ffn_in,
           conv_w, conv_b, w_ffn_out, ffn_post_norm):
    B = x.shape[0]
    depth = w_in.shape[0]
    assert depth == 1, "lambda_init is baked for a single layer"
    outs = []
    for bi in range(B):
        xb = x[bi]
        for l in range(depth):
            xb = _layer(xb, attn_pre_norm[l], w_in[l], w_alpha_up[l], b_alpha[l], gla_norm[l],
                        lambda_q1[l], lambda_k1[l], lambda_q2[l], lambda_k2[l], diff_norm[l],
                        w_o[l], attn_post_norm[l], ffn_pre_norm[l], w_ffn_in[l], conv_w[l],
                        conv_b[l], w_ffn_out[l], ffn_post_norm[l])
        outs.append(xb)
    return outs[0][None] if B == 1 else jnp.stack(outs, axis=0)
```

```python
import functools
import math

import jax
import jax.numpy as jnp
from jax import lax
from jax.experimental import pallas as pl
from jax.experimental.pallas import tpu as pltpu

F32 = jnp.float32
BF16 = jnp.bfloat16

D_MODEL = 2048
GLA_HEADS = 4
GLA_DK = 128
GLA_DV = 256
GLA_RANK = 16
GLA_TAU = 16.0
DIFF_HEADS = 4
DIFF_DQK = 128
DIFF_DV = 256
D_FF = 5632
CONV_W = 3
EPS = 1e-6
LAMBDA_INIT = 0.8 - 0.6 * math.exp(-0.3 * 0)
LOG2E = math.log2(math.e)

GLA_QK = GLA_HEADS * GLA_DK
GLA_V = GLA_HEADS * GLA_DV
DIFF_QK = DIFF_HEADS * 2 * DIFF_DQK
DIFF_V = DIFF_HEADS * DIFF_DV

COL_GQ = 0
COL_GK = COL_GQ + GLA_QK
COL_GV = COL_GK + GLA_QK
COL_GG = COL_GV + GLA_V
COL_DQ = COL_GG + GLA_V
COL_DK = COL_DQ + DIFF_QK
COL_DV = COL_DK + DIFF_QK
PROJ_COLS = COL_DV + DIFF_V
GA_OFFSET = GLA_QK + GLA_QK + GLA_V + GLA_V

V7X_LANES = 128
V7X_SUBLANES = 8
V7X_VMEM_LIMIT_BYTES = 56 * 1024 * 1024

IN_TM = 1024
IN_TN = 1536
OUT_TM = 512
OUT_RH = 256
FFN_TM = 512
FFN_TN = 512
ATT_T = 512
ATT_TQ = 1024
ATT_QW = 256
ATT_SKIP_LOG2 = 160.0
ATT_NORM_SLACK = 1.001
GLA_CHUNK = 64
GLA_ROWS = 256
GLA_SUB = 16
GLA_SAFE_SPAN = 60.0
NEG_BIG = -1e30


def _rms(v, g):
    return v * lax.rsqrt(jnp.mean(v * v, axis=-1, keepdims=True) + EPS) * g


def _params(dims):
    return pltpu.CompilerParams(dimension_semantics=dims,
                                vmem_limit_bytes=V7X_VMEM_LIMIT_BYTES)


def _in_proj_kernel(x_ref, g_ref, w_ref, wga_ref, cs_ref, o_ref, ga_ref, vt_ref, h_scr,
                    *, tn):
    j = pl.program_id(1)

    @pl.when(j == 0)
    def _():
        h = _rms(x_ref[...], g_ref[...]).astype(BF16)
        h_scr[...] = h
        ga_ref[...] = jnp.dot(h, wga_ref[...], preferred_element_type=F32)

    acc = jnp.dot(h_scr[...], w_ref[...], preferred_element_type=F32)
    o_ref[...] = (acc * cs_ref[...]).astype(BF16)

    @pl.when(j == COL_DV // tn)
    def _():
        v0 = COL_DV % tn
        acc_t = acc[:, v0:v0 + DIFF_V].T.astype(BF16)
        for kb in range(vt_ref.shape[0]):
            vt_ref[kb] = acc_t[:, kb * ATT_T:(kb + 1) * ATT_T]


def _in_proj(x, g, w_main, w_ga, colscale):
    S = x.shape[0]
    tm = min(IN_TM, S)
    tn = IN_TN
    return pl.pallas_call(
        functools.partial(_in_proj_kernel, tn=tn),
        grid=(S // tm, PROJ_COLS // tn),
        in_specs=[
            pl.BlockSpec((tm, D_MODEL), lambda i, j: (i, 0)),
            pl.BlockSpec((1, D_MODEL), lambda i, j: (0, 0)),
            pl.BlockSpec((D_MODEL, tn), lambda i, j: (0, j)),
            pl.BlockSpec((D_MODEL, V7X_LANES), lambda i, j: (0, 0)),
            pl.BlockSpec((1, tn), lambda i, j: (0, j)),
        ],
        out_specs=[
            pl.BlockSpec((tm, tn), lambda i, j: (i, j)),
            pl.BlockSpec((tm, V7X_LANES), lambda i, j: (i, 0)),
            pl.BlockSpec((tm // ATT_T, DIFF_V, ATT_T), lambda i, j: (i, 0, 0)),
        ],
        out_shape=[
            jax.ShapeDtypeStruct((S, PROJ_COLS), BF16),
            jax.ShapeDtypeStruct((S, V7X_LANES), F32),
            jax.ShapeDtypeStruct((S // ATT_T, DIFF_V, ATT_T), BF16),
        ],
        scratch_shapes=[pltpu.VMEM((tm, D_MODEL), BF16)],
        compiler_params=_params(("parallel", "arbitrary")),
        name="in_proj",
    )(x, g, w_main, w_ga, colscale)


def _log_sigmoid(x):
    return jnp.minimum(x, 0.0) - jnp.log(1.0 + jnp.exp(-jnp.abs(x)))


def _gla_kernel(q_ref, k_ref, v_ref, gate_ref, ga_ref, gan_ref, wup_ref, ba_ref, gn_ref,
                o_ref, st_scr, b_scr, d_scr, bn_scr, dn_scr, span_scr, a_scr):
    C, R = GLA_CHUNK, GLA_ROWS
    n_chunks = R // C

    row = lax.broadcasted_iota(jnp.int32, (R, R), 0)
    col = lax.broadcasted_iota(jnp.int32, (R, R), 1)
    shift = C.bit_length() - 1
    in_chunk = (col <= row) & ((row >> shift) == (col >> shift))
    tril = jnp.where(in_chunk, 1.0, 0.0).astype(BF16)
    gn = gn_ref[...]

    def decay_prefix(ga_blk):
        z = jnp.dot(ga_blk.astype(BF16), wup_ref[...], preferred_element_type=F32) + ba_ref[...]
        la = _log_sigmoid(z) * (1.0 / GLA_TAU)
        la_hi = la.astype(BF16)
        la_lo = (la - la_hi.astype(F32)).astype(BF16)
        b_all = (jnp.dot(tril, la_hi, preferred_element_type=F32)
                 + jnp.dot(tril, la_lo, preferred_element_type=F32))
        b_first = jnp.concatenate(
            [jnp.broadcast_to(b_all[c * C:c * C + 1], (C, GLA_QK)) for c in range(n_chunks)],
            axis=0)
        d = b_first - b_all
        bn_scr[...] = b_all
        dn_scr[...] = d
        span_scr[0] = jnp.max(d)

    @pl.when(pl.program_id(0) == 0)
    def _():
        st_scr[...] = jnp.zeros_like(st_scr)
        decay_prefix(ga_ref[...])

    b_scr[...] = bn_scr[...]
    d_scr[...] = dn_scr[...]
    small_span = span_scr[0] <= GLA_SAFE_SPAN

    def heads():
        for h in range(GLA_HEADS):
            ks = slice(h * GLA_DK, (h + 1) * GLA_DK)
            yield h, ks, q_ref[:, ks].astype(F32), k_ref[:, ks].astype(F32)

    @pl.when(small_span)
    def _():
        for h, ks, q, k in heads():
            d = d_scr[:, ks]
            q_t = (q * jnp.exp(-d)).astype(BF16)
            k_t = (k * jnp.exp(d)).astype(BF16)
            s = lax.dot_general(q_t, k_t, (((1,), (1,)), ((), ())),
                                preferred_element_type=F32)
            a_scr[h] = jnp.where(in_chunk, s, 0.0)

    @pl.when(jnp.logical_not(small_span))
    def _():
        a_scr[...] = jnp.zeros_like(a_scr)
        sub_row = lax.broadcasted_iota(jnp.int32, (GLA_SUB, C), 0)
        sub_col = lax.broadcasted_iota(jnp.int32, (GLA_SUB, C), 1)
        for h, ks, q_all, k_all in heads():
            for c in range(n_chunks):
                q = q_all[c * C:(c + 1) * C]
                k = k_all[c * C:(c + 1) * C]
                b = b_scr[c * C:(c + 1) * C, ks]
                for blk in range(C // GLA_SUB):
                    r0 = blk * GLA_SUB
                    b_blk = b[r0:r0 + GLA_SUB]
                    q_blk = q[r0:r0 + GLA_SUB]
                    b_ref0 = b[r0:r0 + 1]
                    a_blk = jnp.zeros((GLA_SUB, C), F32)
                    if blk > 0:
                        q_t = (q_blk * jnp.exp(b_blk - b_ref0)).astype(BF16)
                        k_t = (k * jnp.exp(jnp.minimum(b_ref0 - b, 0.0))).astype(BF16)
                        off = lax.dot_general(q_t, k_t, (((1,), (1,)), ((), ())),
                                              preferred_element_type=F32)
                        a_blk = jnp.where(sub_col < r0, off, 0.0)
                    for jj in range(GLA_SUB):
                        kj = k[r0 + jj:r0 + jj + 1]
                        bj = b[r0 + jj:r0 + jj + 1]
                        t = q_blk * kj * jnp.exp(jnp.minimum(b_blk - bj, 0.0))
                        cj = jnp.sum(t, axis=-1, keepdims=True)
                        a_blk = jnp.where((sub_col == r0 + jj) & (sub_row >= jj), cj, a_blk)
                    a_scr[h, c * C + r0:c * C + r0 + GLA_SUB, c * C:(c + 1) * C] = a_blk

    for h, ks, q, k in heads():
        vs = slice(h * GLA_DV, (h + 1) * GLA_DV)
        v = v_ref[:, vs]
        b = b_scr[:, ks]
        q_in = (q * jnp.exp(b)).astype(BF16)
        st = st_scr[h]
        o_inter = []
        for c in range(n_chunks):
            cr = slice(c * C, (c + 1) * C)
            o_inter.append(lax.dot_general(q_in[cr], st.astype(BF16), (((1,), (1,)), ((), ())),
                                           preferred_element_type=F32))
            b_last = b[(c + 1) * C - 1:(c + 1) * C]
            k_out = (k[cr] * jnp.exp(b_last - b[cr])).astype(BF16)
            upd = lax.dot_general(v[cr], k_out, (((0,), (0,)), ((), ())),
                                  preferred_element_type=F32)
            st = st * jnp.exp(b_last) + upd
        st_scr[h] = st
        o = (jnp.concatenate(o_inter, axis=0)
             + jnp.dot(a_scr[h].astype(BF16), v, preferred_element_type=F32))

        gate = gate_ref[:, vs].astype(F32)
        y = _rms(o, gn) * (gate * jax.nn.sigmoid(gate))
        o_ref[:, vs] = y.astype(BF16)

    decay_prefix(gan_ref[...])


def _gla(proj, ga, wup, b_alpha, gla_norm):
    S = proj.shape[0]
    R = GLA_ROWS
    return pl.pallas_call(
        _gla_kernel,
        grid=(S // R,),
        in_specs=[
            pl.BlockSpec((R, GLA_QK), lambda i: (i, COL_GQ // GLA_QK)),
            pl.BlockSpec((R, GLA_QK), lambda i: (i, COL_GK // GLA_QK)),
            pl.BlockSpec((R, GLA_V), lambda i: (i, COL_GV // GLA_V)),
            pl.BlockSpec((R, GLA_V), lambda i: (i, COL_GG // GLA_V)),
            pl.BlockSpec((R, V7X_LANES), lambda i: (i, 0)),
            pl.BlockSpec((R, V7X_LANES), lambda i: (jnp.minimum(i + 1, S // R - 1), 0)),
            pl.BlockSpec((V7X_LANES, GLA_QK), lambda i: (0, 0)),
            pl.BlockSpec((1, GLA_QK), lambda i: (0, 0)),
            pl.BlockSpec((1, GLA_DV), lambda i: (0, 0)),
        ],
        out_specs=pl.BlockSpec((R, GLA_V), lambda i: (i, 0)),
        out_shape=jax.ShapeDtypeStruct((S, GLA_V), BF16),
        scratch_shapes=[
            pltpu.VMEM((GLA_HEADS, GLA_DV, GLA_DK), F32),
            pltpu.VMEM((R, GLA_QK), F32),
            pltpu.VMEM((R, GLA_QK), F32),
            pltpu.VMEM((R, GLA_QK), F32),
            pltpu.VMEM((R, GLA_QK), F32),
            pltpu.SMEM((1,), F32),
            pltpu.VMEM((GLA_HEADS, R, R), F32),
        ],
        compiler_params=_params(("arbitrary",)),
        name="gla",
    )(proj, proj, proj, proj, ga, ga, wup, b_alpha, gla_norm)


def _attn_kernel(slopes_ref, lq1_ref, lk1_ref, lq2_ref, lk2_ref, gn_ref,
                 q_ref, k_ref, vt_ref, o_ref, feat_scr, kn_scr, run_scr, qa_scr, s_scr, m_scr,
                 l_scr, acc_scr):
    TQ, TK, QW, D = q_ref.shape[0], ATT_T, ATT_QW, DIFF_DQK
    h = pl.program_id(0)
    i = pl.program_id(1)
    n_kb = TQ // TK
    lane = lax.broadcasted_iota(jnp.int32, (TK, V7X_LANES), 1)

    slope2 = slopes_ref[h] * LOG2E
    blk_lane = lax.broadcasted_iota(jnp.int32, (1, V7X_LANES), 1)

    n_blocks = feat_scr.shape[0] // TK

    def fill_block(blk):
        r0 = pl.multiple_of(blk * TK, TK)
        row = lax.broadcasted_iota(jnp.int32, (TK, V7X_LANES), 0)
        b = slope2 * (row + r0).astype(F32)
        hi = b.astype(BF16).astype(F32)
        r1 = b - hi
        mid = r1.astype(BF16).astype(F32)
        lo = r1 - mid
        feat = jnp.where(lane == 0, hi,
                         jnp.where(lane == 1, mid, jnp.where(lane == 2, lo, 0.0)))
        feat_scr[pl.ds(r0, TK), :] = feat.astype(BF16)
        for c in range(2):
            kc = k_ref[pl.ds(r0, TK), c * D:(c + 1) * D].astype(F32)
            sq = jnp.sum(kc * kc, axis=-1, keepdims=True)
            run = jnp.maximum(run_scr[c], jnp.max(sq, axis=0, keepdims=True))
            run_scr[c] = run
            kn_scr[c] = jnp.where(blk_lane == blk, jnp.sqrt(run), kn_scr[c])

    @pl.when(i == 0)
    def _():
        run_scr[...] = jnp.zeros_like(run_scr)
        kn_scr[...] = jnp.zeros_like(kn_scr)
        fill_block(0)
        fill_block(1)

    m_scr[...] = jnp.full_like(m_scr, NEG_BIG)
    l_scr[...] = jnp.zeros_like(l_scr)
    acc_scr[...] = jnp.zeros_like(acc_scr)

    q_lane = lax.broadcasted_iota(jnp.int32, (TQ, V7X_LANES), 1)
    q_ones = jnp.where(q_lane < 3, 1.0, 0.0).astype(BF16)
    for c in range(2):
        qa_scr[c, :, 0:D] = q_ref[:, c * D:(c + 1) * D]
        qa_scr[c, :, D:2 * D] = q_ones
    key_row = lax.broadcasted_iota(jnp.int32, (TK, QW), 0)
    query_col = lax.broadcasted_iota(jnp.int32, (TK, QW), 1)

    def slabs(key_off):
        return [sl for sl in range(TQ // QW)
                if key_off is None or key_off <= sl * QW + QW - 1]

    def scores(j, slot, key_off):
        r0 = pl.multiple_of(j * TK, TK)
        feat = feat_scr[pl.ds(r0, TK), :]
        for c in range(2):
            k = jnp.concatenate([k_ref[pl.ds(r0, TK), c * D:(c + 1) * D], feat], axis=1)
            for sl in slabs(key_off):
                qs = slice(sl * QW, (sl + 1) * QW)
                s_scr[slot, c, :, qs] = lax.dot_general(
                    k, qa_scr[c, qs, :], (((1,), (1,)), ((), ())),
                    preferred_element_type=F32)

    def accumulate(j, slot, key_off):
        vt = vt_ref[j]
        for c in range(2):
            for sl in slabs(key_off):
                q_lo = sl * QW
                qs = slice(q_lo, q_lo + QW)
                s = s_scr[slot, c, :, qs]
                if key_off is not None and key_off + TK - 1 > q_lo:
                    s = jnp.where(key_row + key_off <= query_col + q_lo, s, NEG_BIG)
                m_prev = m_scr[c, :, qs]
                m_new = jnp.maximum(m_prev, jnp.max(s, axis=0, keepdims=True))
                alpha = jnp.exp2(m_prev - m_new)
                p = jnp.exp2(s - m_new)
                l_scr[c, :, qs] = alpha * l_scr[c, :, qs] + jnp.sum(p, axis=0, keepdims=True)
                acc_scr[c, :, qs] = alpha * acc_scr[c, :, qs] + jnp.dot(
                    vt, p.astype(BF16), preferred_element_type=F32)
                m_scr[c, :, qs] = m_new

    assert n_kb == 2
    scores(2 * i, 0, 0)
    scores(2 * i + 1, 1, TK)
    nxt = jnp.minimum(2 * i + 2, n_blocks - 2)
    fill_block(nxt)
    fill_block(nxt + 1)
    accumulate(2 * i + 1, 1, TK)
    scores(jnp.maximum(2 * i - 1, 0), 1, None)
    accumulate(2 * i, 0, 0)

    last_pos = (blk_lane * TK + (TK - 1)).astype(F32)
    skip = blk_lane < 2 * i
    for c in range(2):
        qf = q_ref[:, c * D:(c + 1) * D].astype(F32)
        qn = jnp.sqrt(jnp.max(jnp.sum(qf * qf, axis=-1, keepdims=True), axis=0, keepdims=True))
        bound = qn * kn_scr[c] * ATT_NORM_SLACK + slope2 * last_pos
        m_min = jnp.min(m_scr[c], axis=-1, keepdims=True)
        skip = skip & (bound < m_min - ATT_SKIP_LOG2)
    first_kept = jnp.min(jnp.where(skip, float(V7X_LANES), blk_lane.astype(F32)))
    n_skip = jnp.minimum(first_kept.astype(jnp.int32), 2 * i)
    n_pairs = i - n_skip // 2

    def pair(lo, prefetch):
        scores(lo, 0, None)
        accumulate(lo + 1, 1, None)
        if prefetch:
            scores(lo - 1, 1, None)
        accumulate(lo, 0, None)

    def body(t, carry):
        pair(2 * (i - 1 - t), True)
        return carry

    lax.fori_loop(0, n_pairs - 1, body, 0)

    @pl.when(n_pairs > 0)
    def _():
        pair(2 * (i - n_pairs), False)

    lam = (jnp.exp(jnp.sum(lq1_ref[...] * lk1_ref[...], axis=-1, keepdims=True))
           - jnp.exp(jnp.sum(lq2_ref[...] * lk2_ref[...], axis=-1, keepdims=True))
           + LAMBDA_INIT)
    o1 = acc_scr[0] / l_scr[0]
    o2 = acc_scr[1] / l_scr[1]
    o = (o1 - lam * o2).T
    o_ref[...] = (_rms(o, gn_ref[...]) * (1.0 - LAMBDA_INIT)).astype(BF16)


def _diff_attn(proj, vt, slopes, lq1, lk1, lq2, lk2, diff_norm):
    S = proj.shape[0]
    TQ = min(ATT_TQ, S)
    W = 2 * DIFF_DQK
    vec = pl.BlockSpec((1, DIFF_DQK), lambda h, i: (0, 0))
    return pl.pallas_call(
        _attn_kernel,
        grid=(DIFF_HEADS, S // TQ),
        in_specs=[
            pl.BlockSpec(memory_space=pltpu.SMEM),
            vec, vec, vec, vec,
            pl.BlockSpec((1, DIFF_DV), lambda h, i: (0, 0)),
            pl.BlockSpec((TQ, W), lambda h, i: (i, COL_DQ // W + h)),
            pl.BlockSpec((S, W), lambda h, i: (0, COL_DK // W + h)),
            pl.BlockSpec((S // ATT_T, DIFF_DV, ATT_T), lambda h, i: (0, h, 0)),
        ],
        out_specs=pl.BlockSpec((TQ, DIFF_DV), lambda h, i: (i, h)),
        out_shape=jax.ShapeDtypeStruct((S, DIFF_V), BF16),
        scratch_shapes=[
            pltpu.VMEM((S, V7X_LANES), BF16),
            pltpu.VMEM((2, 1, V7X_LANES), F32),
            pltpu.VMEM((2, 1, V7X_LANES), F32),
            pltpu.VMEM((2, TQ, W), BF16),
            pltpu.VMEM((2, 2, ATT_T, TQ), F32),
            pltpu.VMEM((2, 1, TQ), F32),
            pltpu.VMEM((2, 1, TQ), F32),
            pltpu.VMEM((2, DIFF_DV, TQ), F32),
        ],
        compiler_params=_params(("arbitrary", "arbitrary")),
        name="diff_attn",
    )(slopes, lq1, lk1, lq2, lk2, diff_norm, proj, proj, vt)


def _out_proj_kernel(oa_ref, ob_ref, wo_ref, x_ref, gpost_ref, gpre_ref,
                     x1_ref, h2_ref):
    for r0 in range(0, x_ref.shape[0], OUT_RH):
        rs = slice(r0, r0 + OUT_RH)
        m = (jnp.dot(oa_ref[rs, :], wo_ref[0:GLA_V, :], preferred_element_type=F32)
             + jnp.dot(ob_ref[rs, :], wo_ref[GLA_V:GLA_V + DIFF_V, :],
                       preferred_element_type=F32))
        x1 = x_ref[rs, :] + _rms(m, gpost_ref[...])
        x1_ref[rs, :] = x1
        h2_ref[rs, :] = _rms(x1, gpre_ref[...]).astype(BF16)


def _out_proj(o_a, o_b, w_o, x, g_post, g_pre):
    S = x.shape[0]
    tm = min(OUT_TM, S)
    row = lambda i: (i, 0)
    fixed = lambda i: (0, 0)
    return pl.pallas_call(
        _out_proj_kernel,
        grid=(S // tm,),
        in_specs=[
            pl.BlockSpec((tm, GLA_V), row),
            pl.BlockSpec((tm, DIFF_V), row),
            pl.BlockSpec((GLA_V + DIFF_V, D_MODEL), fixed),
            pl.BlockSpec((tm, D_MODEL), row),
            pl.BlockSpec((1, D_MODEL), fixed),
            pl.BlockSpec((1, D_MODEL), fixed),
        ],
        out_specs=[
            pl.BlockSpec((tm, D_MODEL), row),
            pl.BlockSpec((tm, D_MODEL), row),
        ],
        out_shape=[
            jax.ShapeDtypeStruct((S, D_MODEL), F32),
            jax.ShapeDtypeStruct((S, D_MODEL), BF16),
        ],
        compiler_params=_params(("parallel",)),
        name="out_proj",
    )(o_a, o_b, w_o, x, g_post, g_pre)


def _gelu_tanh(x):
    k = -2.0 * math.sqrt(2.0 / math.pi) * LOG2E
    return x / (1.0 + jnp.exp2(x * (k + (k * 0.044715) * (x * x))))


def _ffn_up_kernel(h2_ref, wa_ref, wb_ref, cw_ref, cb_ref, g_ref, abuf_scr, halo_scr, *, tm):
    i = pl.program_id(0)
    j = pl.program_id(1)
    P = V7X_SUBLANES

    @pl.when((i == 0) & (j == 0))
    def _():
        halo_scr[...] = jnp.zeros_like(halo_scr)

    abuf_scr[0:P, :] = halo_scr[j]
    h2 = h2_ref[...]
    a = jnp.dot(h2, wa_ref[...], preferred_element_type=F32)
    b = jnp.dot(h2, wb_ref[...], preferred_element_type=F32)
    abuf_scr[P:P + tm, :] = a
    a1 = abuf_scr[P - 1:P - 1 + tm, :]
    a2 = abuf_scr[P - 2:P - 2 + tm, :]
    conv = cw_ref[0:1, :] * a2 + cw_ref[1:2, :] * a1 + cw_ref[2:3, :] * a + cb_ref[...]
    g_ref[...] = (_gelu_tanh(conv) * b).astype(BF16)
    halo_scr[j] = abuf_scr[tm:tm + P, :]


def _ffn_down_kernel(g_ref, wout_ref, x1_ref, gpost_ref, o_ref):
    for r0 in range(0, g_ref.shape[0], OUT_RH):
        rs = slice(r0, r0 + OUT_RH)
        f = jnp.dot(g_ref[rs, :], wout_ref[...], preferred_element_type=F32)
        o_ref[rs, :] = x1_ref[rs, :] + _rms(f, gpost_ref[...])


def _ffn(h2, w_ffn_in, conv_w, conv_b, w_ffn_out, x1, g_post):
    S = h2.shape[0]
    tm = min(FFN_TM, S)
    tn = FFN_TN
    nj = D_FF // tn
    row = lambda i, j: (i, 0)
    g = pl.pallas_call(
        functools.partial(_ffn_up_kernel, tm=tm),
        grid=(S // tm, nj),
        in_specs=[
            pl.BlockSpec((tm, D_MODEL), row),
            pl.BlockSpec((D_MODEL, tn), lambda i, j: (0, j)),
            pl.BlockSpec((D_MODEL, tn), lambda i, j: (0, nj + j)),
            pl.BlockSpec((CONV_W, tn), lambda i, j: (0, j)),
            pl.BlockSpec((1, tn), lambda i, j: (0, j)),
        ],
        out_specs=pl.BlockSpec((tm, tn), lambda i, j: (i, j)),
        out_shape=jax.ShapeDtypeStruct((S, D_FF), BF16),
        scratch_shapes=[
            pltpu.VMEM((tm + V7X_SUBLANES, tn), F32),
            pltpu.VMEM((nj, V7X_SUBLANES, tn), F32),
        ],
        compiler_params=_params(("arbitrary", "arbitrary")),
        name="ffn_up",
    )(h2, w_ffn_in, w_ffn_in, conv_w, conv_b)
    return pl.pallas_call(
        _ffn_down_kernel,
        grid=(S // tm,),
        in_specs=[
            pl.BlockSpec((tm, D_FF), lambda i: (i, 0)),
            pl.BlockSpec((D_FF, D_MODEL), lambda i: (0, 0), pipeline_mode=pl.Buffered(1)),
            pl.BlockSpec((tm, D_MODEL), lambda i: (i, 0)),
            pl.BlockSpec((1, D_MODEL), lambda i: (0, 0)),
        ],
        out_specs=pl.BlockSpec((tm, D_MODEL), lambda i: (i, 0)),
        out_shape=jax.ShapeDtypeStruct((S, D_MODEL), F32),
        compiler_params=_params(("parallel",)),
        name="ffn_down",
    )(g, w_ffn_out, x1, g_post)


def _layer(x, attn_pre_norm, w_in, w_alpha_up, b_alpha, gla_norm, lambda_q1, lambda_k1,
           lambda_q2, lambda_k2, diff_norm, w_o, attn_post_norm, ffn_pre_norm, w_ffn_in,
           conv_w, conv_b, w_ffn_out, ffn_post_norm):
    vec = lambda p: p.reshape(1, -1).astype(F32)

    w_in16 = w_in.astype(BF16)
    w_main = jnp.concatenate(
        [w_in16[:, :GA_OFFSET], w_in16[:, GA_OFFSET + GLA_RANK:]], axis=1)
    w_ga = jnp.pad(w_in16[:, GA_OFFSET:GA_OFFSET + GLA_RANK],
                   ((0, 0), (0, V7X_LANES - GLA_RANK)))
    colscale = jnp.concatenate([
        jnp.full((GLA_QK,), GLA_DK ** -0.5, F32),
        jnp.ones((COL_DQ - COL_GK,), F32),
        jnp.full((DIFF_QK,), DIFF_DQK ** -0.5 * LOG2E, F32),
        jnp.ones((PROJ_COLS - COL_DK,), F32),
    ]).reshape(1, PROJ_COLS)
    wup = jnp.pad(w_alpha_up, ((0, V7X_LANES - GLA_RANK), (0, 0))).astype(BF16)
    slopes = jnp.asarray(
        [2.0 ** (-8.0 * (h + 1) / DIFF_HEADS) for h in range(DIFF_HEADS)], F32)

    proj, ga, vt = _in_proj(x, vec(attn_pre_norm), w_main, w_ga, colscale)
    o_a = _gla(proj, ga, wup, vec(b_alpha), vec(gla_norm))
    o_b = _diff_attn(proj, vt, slopes, vec(lambda_q1), vec(lambda_k1), vec(lambda_q2),
                     vec(lambda_k2), vec(diff_norm))
    x1, h2 = _out_proj(o_a, o_b, w_o.astype(BF16), x, vec(attn_post_norm),
                       vec(ffn_pre_norm))
    return _ffn(h2, w_ffn_in.astype(BF16), conv_w.astype(F32), vec(conv_b),
                w_ffn_out.astype(BF16), x1, vec(ffn_post_norm))


def kernel(x, attn_pre_norm, w_in, w_alpha_up, b_alpha, gla_norm, lambda_q1, lambda_k1,
           lambda_q2, lambda_k2, diff_norm, w_o, attn_post_norm, ffn_pre_norm, w_ffn_in,
           conv_w, conv_b, w_ffn_out, ffn_post_norm):
    B = x.shape[0]
    depth = w_in.shape[0]
    assert depth == 1, "lambda_init is baked for a single layer"
    outs = []
    for bi in range(B):
        xb = x[bi]
        for l in range(depth):
            xb = _layer(xb, attn_pre_norm[l], w_in[l], w_alpha_up[l], b_alpha[l], gla_norm[l],
                        lambda_q1[l], lambda_k1[l], lambda_q2[l], lambda_k2[l], diff_norm[l],
                        w_o[l], attn_post_norm[l], ffn_pre_norm[l], w_ffn_in[l], conv_w[l],
                        conv_b[l], w_ffn_out[l], ffn_post_norm[l])
        outs.append(xb)
    return outs[0][None] if B == 1 else jnp.stack(outs, axis=0)
```
